```python
import math
import jax
import jax.numpy as jnp
from jax import lax
import numpy as np

D_MODEL = 1024
BATCH = 8
SEQ = 8192
DEPTH = 2
DEC_BATCH = 8
DEC_SEQ = 4096
PAST_LEN = 128

GROUP_W = D_MODEL // 4
D_MIX = 4 * GROUP_W
CONV_W = 4
CONV_PAD = (2, 1)
EPS = 1e-6
RG_BLOCKS = 4
RG_BLOCK_DIM = GROUP_W // RG_BLOCKS
RG_C = 8.0
HG_HEADS = 4
HG_KEY_DIM = GROUP_W // HG_HEADS
HG_CHUNK = 64
SSD_HEADS = 4
SSD_HEAD_DIM = GROUP_W // SSD_HEADS
SSD_GROUPS = 2
SSD_STATE = 128
SSD_XBC = GROUP_W + 2 * SSD_GROUPS * SSD_STATE
SSD_CHUNK = 128
RET_HEADS = 4
RET_HEAD_DIM = GROUP_W // RET_HEADS
RET_CHUNK = 128
RET_DECAY_EXP_FWD = 5.0
RET_DECAY_EXP_BWD = 5.5
ROPE_BASE = 10000.0
N_EXPERTS = 16
EXPERT_FF = 2 * D_MODEL
EC_CAPACITY_FACTOR = 2
SPLIT_SIZES = (GROUP_W, GROUP_W,
               GROUP_W, GROUP_W, GROUP_W, GROUP_W, GROUP_W,
               GROUP_W, SSD_XBC, 2 * SSD_HEADS,
               GROUP_W, GROUP_W, GROUP_W, GROUP_W)
IN_COLS = sum(SPLIT_SIZES)
SPLIT_POINTS = tuple(int(p) for p in np.cumsum(SPLIT_SIZES)[:-1])

kernel_name = 'hybrid_bidir_hymba_ec_encoder'


def _rmsnorm(x, w):
    xf = x.astype(jnp.float32)
    y = xf * lax.rsqrt(jnp.mean(xf * xf, axis=-1, keepdims=True) + EPS)
    return (y * w.astype(jnp.float32)).astype(x.dtype)


def _head_rmsnorm(x):
    xf = x.astype(jnp.float32)
    return xf * lax.rsqrt(jnp.mean(xf * xf, axis=-1, keepdims=True) + EPS)


def _flip(t):
    return jnp.flip(t, axis=1)


def _dwconv_centred(x, w, bias):
    y = lax.conv_general_dilated(x, w[:, None, :].astype(x.dtype), window_strides=(1,),
                                 padding=[CONV_PAD], dimension_numbers=('NWC', 'WIO', 'NWC'),
                                 feature_group_count=x.shape[-1])
    return y + bias.astype(x.dtype)


def _rotary(x, pos):
    half = x.shape[-1] // 2
    inv_freq = ROPE_BASE ** (-jnp.arange(half, dtype=jnp.float32) / half)
    ang = pos[:, None] * inv_freq[None, :]
    cos = jnp.cos(ang)[None, :, None, :]
    sin = jnp.sin(ang)[None, :, None, :]
    x1, x2 = x[..., :half], x[..., half:]
    return jnp.concatenate([x1 * cos - x2 * sin, x2 * cos + x1 * sin], axis=-1)


def _linear_scan(a, b, reverse):
    def combine(c1, c2):
        a1, b1 = c1
        a2, b2 = c2
        return a1 * a2, a2 * b1 + b2
    return lax.associative_scan(combine, (a, b), reverse=reverse, axis=1)[1]


def _scalar_decay_chunked(q, k, v, la, chunk):
    b, s, h, dk = q.shape
    dv = v.shape[-1]
    n = s // chunk
    qc = q.reshape(b, n, chunk, h, dk)
    kc = k.reshape(b, n, chunk, h, dk)
    vc = v.reshape(b, n, chunk, h, dv)
    cum = jnp.cumsum(la.reshape(b, n, chunk, h), axis=2)
    cum_h = jnp.moveaxis(cum, 3, 2)
    lower = jnp.tril(jnp.ones((chunk, chunk), dtype=bool))
    seg = cum_h[..., :, None] - cum_h[..., None, :]
    decay = jnp.where(lower, jnp.exp(jnp.where(lower, seg, 0.0)), 0.0)
    scores = jnp.einsum('bnthd,bnshd->bnhts', qc, kc) * decay
    intra = jnp.einsum('bnhts,bnshv->bnthv', scores, vc)
    last = cum_h[..., -1]
    k_to_end = kc * jnp.exp(last[:, :, None, :] - cum)[..., None]
    chunk_state = jnp.einsum('bnshd,bnshv->bnhdv', k_to_end, vc)

    def carry_state(state, inp):
        st, dl = inp
        return jnp.exp(dl)[..., None, None] * state + st, state

    _, s_in = lax.scan(carry_state, jnp.zeros((b, h, dk, dv), q.dtype),
                       (jnp.moveaxis(chunk_state, 1, 0), jnp.moveaxis(last, 1, 0)))
    s_in = jnp.moveaxis(s_in, 0, 1)
    inter = jnp.einsum('bnthd,bnhdv->bnthv', qc * jnp.exp(cum)[..., None], s_in)
    return (intra + inter).reshape(b, s, h, dv)


def _gla_chunked(q, k, v, lf):
    b, s, h, dk = q.shape
    dv = v.shape[-1]
    n = s // HG_CHUNK
    lower = jnp.tril(jnp.ones((HG_CHUNK, HG_CHUNK), dtype=bool))[None, :, :, None, None]

    def to_chunks(t):
        return jnp.moveaxis(t.reshape(b, n, HG_CHUNK, h, t.shape[-1]), 1, 0)

    def chunk_step(state, inp):
        qc, kc, vc, lc = inp
        cum = jnp.cumsum(lc, axis=1)
        seg = cum[:, :, None] - cum[:, None, :]
        decay = jnp.where(lower, jnp.exp(jnp.where(lower, seg, 0.0)), 0.0)
        scores = jnp.einsum('bthd,btshd->bhts', qc, decay * kc[:, None])
        last = cum[:, -1]
        out = (jnp.einsum('bhts,bshv->bthv', scores, vc)
               + jnp.einsum('bthd,bhdv->bthv', qc * jnp.exp(cum), state))
        state = (jnp.exp(last)[..., None] * state
                 + jnp.einsum('bshd,bshv->bhdv', kc * jnp.exp(last[:, None] - cum), vc))
        return state, out

    _, out = lax.scan(chunk_step, jnp.zeros((b, h, dk, dv), q.dtype),
                      (to_chunks(q), to_chunks(k), to_chunks(v), to_chunks(lf)))
    return jnp.moveaxis(out, 0, 1).reshape(b, s, h, dv)


def _rglru_mixer(xa, ga, conv_w, conv_b, wa, ba, wx, bx, lam):
    u = _dwconv_centred(xa, conv_w, conv_b)
    b, s, w = u.shape
    ub = u.reshape(b, s, RG_BLOCKS, RG_BLOCK_DIM)
    r = jax.nn.sigmoid((jnp.einsum('bsnk,rnkj->rbsnj', ub, wa).reshape(2, b, s, w)
                        + ba[:, None, None, :]).astype(jnp.float32))
    i = jax.nn.sigmoid((jnp.einsum('bsnk,rnkj->rbsnj', ub, wx).reshape(2, b, s, w)
                        + bx[:, None, None, :]).astype(jnp.float32))
    log_a = -RG_C * r * jax.nn.softplus(-lam.astype(jnp.float32))[:, None, None, :]
    a = jnp.exp(log_a)
    inp = jnp.sqrt(-jnp.expm1(2.0 * log_a)) * i * u.astype(jnp.float32)[None]
    h = _linear_scan(a[0], inp[0], False) + _linear_scan(a[1], inp[1], True)
    return h.astype(xa.dtype) * jax.nn.gelu(ga)


def _hgrn2_mixer(q, f_fwd, f_bwd, i, g, lb_logits, layer, norm_w):
    b, s, w = q.shape
    lb_all = jnp.cumsum(jax.nn.softmax(lb_logits.astype(jnp.float32), axis=0), axis=0)
    lb = lb_all[layer] - lb_all[0]

    def heads(t):
        return t.astype(jnp.float32).reshape(b, s, HG_HEADS, HG_KEY_DIM)

    def log_forget(z, lbd):
        f = lbd + (1.0 - lbd) * jax.nn.sigmoid(z.astype(jnp.float32))
        return jnp.log(f).reshape(b, s, HG_HEADS, HG_KEY_DIM)

    qh, ih = heads(q), heads(i)
    lf_f = log_forget(f_fwd, lb[0])
    lf_b = log_forget(f_bwd, lb[1])
    o = (_gla_chunked(qh, -jnp.expm1(lf_f), ih, lf_f)
         + _flip(_gla_chunked(_flip(qh), _flip(-jnp.expm1(lf_b)), _flip(ih), _flip(lf_b))))
    o = _head_rmsnorm(o).reshape(b, s, w) * norm_w.astype(jnp.float32)
    return o * jax.nn.silu(g.astype(jnp.float32))


def _ssd_mixer(z, xbc, dt_raw, conv_w, conv_b, dt_bias, a_log, d_skip, norm_w):
    b, s, _ = z.shape
    xbc = jax.nn.silu(_dwconv_centred(xbc, conv_w, conv_b))
    gn = SSD_GROUPS * SSD_STATE
    xs = xbc[..., :GROUP_W]
    bm = xbc[..., GROUP_W:GROUP_W + gn]
    cm = xbc[..., GROUP_W + gn:]
    rep = SSD_HEADS // SSD_GROUPS
    xh = xs.astype(jnp.float32).reshape(b, s, SSD_HEADS, SSD_HEAD_DIM)
    bh = jnp.repeat(bm.astype(jnp.float32).reshape(b, s, SSD_GROUPS, SSD_STATE), rep, axis=2)
    ch = jnp.repeat(cm.astype(jnp.float32).reshape(b, s, SSD_GROUPS, SSD_STATE), rep, axis=2)
    dt = jax.nn.softplus(dt_raw.astype(jnp.float32).reshape(b, s, 2, SSD_HEADS)
                         + dt_bias.astype(jnp.float32))
    la = dt * (-jnp.exp(a_log.astype(jnp.float32)))
    v_f = xh * dt[:, :, 0, :, None]
    v_b = xh * dt[:, :, 1, :, None]
    y = (_scalar_decay_chunked(ch, bh, v_f, la[:, :, 0], SSD_CHUNK)
         + _flip(_scalar_decay_chunked(_flip(ch), _flip(bh), _flip(v_b), _flip(la[:, :, 1]), SSD_CHUNK)))
    y = y + d_skip.astype(jnp.float32)[:, None] * xh
    y = y.reshape(b, s, GROUP_W) * jax.nn.silu(z.astype(jnp.float32))
    return _rmsnorm(y, norm_w)


def _retention_mixer(q, k, v, g):
    b, s, w = q.shape
    pos = jnp.arange(s, dtype=jnp.float32)
    qh = _rotary(q.astype(jnp.float32).reshape(b, s, RET_HEADS, RET_HEAD_DIM), pos)
    kh = _rotary(k.astype(jnp.float32).reshape(b, s, RET_HEADS, RET_HEAD_DIM), pos) * RET_HEAD_DIM ** -0.5
    vh = v.astype(jnp.float32).reshape(b, s, RET_HEADS, RET_HEAD_DIM)
    hidx = jnp.arange(RET_HEADS, dtype=jnp.float32)
    la_f = jnp.broadcast_to(jnp.log1p(-jnp.exp2(-RET_DECAY_EXP_FWD - hidx)), (b, s, RET_HEADS))
    la_b = jnp.broadcast_to(jnp.log1p(-jnp.exp2(-RET_DECAY_EXP_BWD - hidx)), (b, s, RET_HEADS))
    o = (_scalar_decay_chunked(qh, kh, vh, la_f, RET_CHUNK)
         + _flip(_scalar_decay_chunked(_flip(qh), _flip(kh), _flip(vh), la_b, RET_CHUNK)))
    return _head_rmsnorm(o).reshape(b, s, w) * jax.nn.silu(g.astype(jnp.float32))


def _expert_choice_ffn(h, router_w, w_gate, w_up, w_down):
    b, s, d = h.shape
    t = h.reshape(b * s, d)
    cap = (EC_CAPACITY_FACTOR * b * s) // N_EXPERTS
    aff = jax.nn.softmax(jnp.einsum('td,de->te', t, router_w).astype(jnp.float32), axis=-1)
    gates, idx = lax.top_k(aff.T, cap)

    def expert(args):
        wg, wu, wd, ix, gt = args
        xe = t[ix]
        he = jax.nn.silu(xe @ wg) * (xe @ wu)
        return (he @ wd) * gt[:, None].astype(t.dtype)

    out = lax.map(expert, (w_gate, w_up, w_down, idx, gates))
    y = jnp.zeros_like(t).at[idx.reshape(-1)].add(out.reshape(-1, d))
    return y.reshape(b, s, d)


def _trunk(x, norm_mix, w_in, rg_conv_w, rg_conv_b, rg_wa, rg_ba, rg_wx, rg_bx, rg_lambda,
           hg_lb_logits, hg_norm_w, ssd_conv_w, ssd_conv_b, ssd_dt_bias, ssd_a_log, ssd_d,
           ssd_norm_w, w_out, norm_ffn, router_w, exp_w_gate, exp_w_up, exp_w_down, norm_final):
    for l in range(DEPTH):
        h = _rmsnorm(x, norm_mix[l])
        u = jnp.einsum('bsd,dc->bsc', h, w_in[l])
        (xa, ga, qb, ffb, fbb, ib, gb, zc, xbcc, dtc, qd, kd, vd, gd) = jnp.split(u, SPLIT_POINTS, axis=-1)
        o_a = _rglru_mixer(xa, ga, rg_conv_w[l], rg_conv_b[l], rg_wa[l], rg_ba[l], rg_wx[l], rg_bx[l], rg_lambda[l])
        o_b = _hgrn2_mixer(qb, ffb, fbb, ib, gb, hg_lb_logits, l, hg_norm_w[l])
        o_c = _ssd_mixer(zc, xbcc, dtc, ssd_conv_w[l], ssd_conv_b[l], ssd_dt_bias[l], ssd_a_log[l], ssd_d[l], ssd_norm_w[l])
        o_d = _retention_mixer(qd, kd, vd, gd)
        mix = jnp.concatenate([o_a.astype(x.dtype), o_b.astype(x.dtype), o_c.astype(x.dtype), o_d.astype(x.dtype)], axis=-1)
        x = x + jnp.einsum('bsc,cd->bsd', mix, w_out[l])
        h = _rmsnorm(x, norm_ffn[l])
        x = x + _expert_choice_ffn(h, router_w[l], exp_w_gate[l], exp_w_up[l], exp_w_down[l])
    return _rmsnorm(x, norm_final)


def setup_inputs(seed: int = 0) -> dict:
    key = jax.random.key(seed)
    ks = jax.random.split(key, 26)
    f32 = jnp.float32

    def nrm(k, shape, scale):
        return jax.random.normal(k, shape, f32) * scale

    x_prompt = nrm(ks[0], (BATCH, SEQ, D_MODEL), 1.0)
    x_sample = nrm(ks[1], (DEC_BATCH, DEC_SEQ, D_MODEL), 1.0)
    norm_mix = 1.0 + nrm(ks[2], (DEPTH, D_MODEL), 0.02)
    w_in = nrm(ks[3], (DEPTH, D_MODEL, IN_COLS), D_MODEL ** -0.5)
    rg_conv_w = nrm(ks[4], (DEPTH, CONV_W, GROUP_W), CONV_W ** -0.5)
    rg_conv_b = nrm(ks[5], (DEPTH, GROUP_W), 0.01)
    rg_wa = nrm(ks[6], (DEPTH, 2, RG_BLOCKS, RG_BLOCK_DIM, RG_BLOCK_DIM), RG_BLOCK_DIM ** -0.5)
    rg_ba = nrm(ks[7], (DEPTH, 2, GROUP_W), 0.01)
    rg_wx = nrm(ks[8], (DEPTH, 2, RG_BLOCKS, RG_BLOCK_DIM, RG_BLOCK_DIM), RG_BLOCK_DIM ** -0.5)
    rg_bx = nrm(ks[9], (DEPTH, 2, GROUP_W), 0.01)
    a0 = jax.random.uniform(ks[10], (DEPTH, 2, GROUP_W), f32, 0.9, 0.999)
    p = a0 ** (1.0 / RG_C)
    rg_lambda = jnp.log(p) - jnp.log1p(-p)
    hg_lb_logits = nrm(ks[11], (DEPTH, 2, GROUP_W), 1.0)
    hg_norm_w = 1.0 + nrm(ks[12], (DEPTH, GROUP_W), 0.02)
    ssd_conv_w = nrm(ks[13], (DEPTH, CONV_W, SSD_XBC), CONV_W ** -0.5)
    ssd_conv_b = nrm(ks[14], (DEPTH, SSD_XBC), 0.01)
    dt0 = jnp.exp(jax.random.uniform(ks[15], (DEPTH, 2, SSD_HEADS), f32, math.log(1e-3), math.log(1e-1)))
    ssd_dt_bias = dt0 + jnp.log(-jnp.expm1(-dt0))
    ssd_a_log = jnp.log(jax.random.uniform(ks[16], (DEPTH, 2, SSD_HEADS), f32, 1.0, 16.0))
    ssd_d = 1.0 + nrm(ks[17], (DEPTH, SSD_HEADS), 0.02)
    ssd_norm_w = 1.0 + nrm(ks[18], (DEPTH, GROUP_W), 0.02)
    w_out = nrm(ks[19], (DEPTH, D_MIX, D_MODEL), D_MIX ** -0.5)
    norm_ffn = 1.0 + nrm(ks[20], (DEPTH, D_MODEL), 0.02)
    router_w = nrm(ks[21], (DEPTH, D_MODEL, N_EXPERTS), D_MODEL ** -0.5)
    exp_w_gate = nrm(ks[22], (DEPTH, N_EXPERTS, D_MODEL, EXPERT_FF), D_MODEL ** -0.5)
    exp_w_up = nrm(ks[23], (DEPTH, N_EXPERTS, D_MODEL, EXPERT_FF), D_MODEL ** -0.5)
    exp_w_down = nrm(ks[24], (DEPTH, N_EXPERTS, EXPERT_FF, D_MODEL), EXPERT_FF ** -0.5)
    norm_final = 1.0 + nrm(ks[25], (D_MODEL,), 0.02)
    return {'x_prompt': x_prompt, 'x_sample': x_sample, 'norm_mix': norm_mix, 'w_in': w_in,
            'rg_conv_w': rg_conv_w, 'rg_conv_b': rg_conv_b, 'rg_wa': rg_wa, 'rg_ba': rg_ba,
            'rg_wx': rg_wx, 'rg_bx': rg_bx, 'rg_lambda': rg_lambda, 'hg_lb_logits': hg_lb_logits,
            'hg_norm_w': hg_norm_w, 'ssd_conv_w': ssd_conv_w, 'ssd_conv_b': ssd_conv_b,
            'ssd_dt_bias': ssd_dt_bias, 'ssd_a_log': ssd_a_log, 'ssd_d': ssd_d,
            'ssd_norm_w': ssd_norm_w, 'w_out': w_out, 'norm_ffn': norm_ffn, 'router_w': router_w,
            'exp_w_gate': exp_w_gate, 'exp_w_up': exp_w_up, 'exp_w_down': exp_w_down,
            'norm_final': norm_final}


def reference(x_prompt, x_sample, norm_mix, w_in, rg_conv_w, rg_conv_b, rg_wa, rg_ba, rg_wx, rg_bx,
              rg_lambda, hg_lb_logits, hg_norm_w, ssd_conv_w, ssd_conv_b, ssd_dt_bias, ssd_a_log,
              ssd_d, ssd_norm_w, w_out, norm_ffn, router_w, exp_w_gate, exp_w_up, exp_w_down,
              norm_final):
    params = (norm_mix, w_in, rg_conv_w, rg_conv_b, rg_wa, rg_ba, rg_wx, rg_bx, rg_lambda,
              hg_lb_logits, hg_norm_w, ssd_conv_w, ssd_conv_b, ssd_dt_bias, ssd_a_log, ssd_d,
              ssd_norm_w, w_out, norm_ffn, router_w, exp_w_gate, exp_w_up, exp_w_down, norm_final)
    y_prompt = _trunk(x_prompt, *params)
    y_sample = _trunk(x_sample, *params)
    return (y_prompt, y_sample)
```

```python
import functools
import math

import jax
import jax.numpy as jnp
import numpy as np
from jax import lax
from jax.experimental import pallas as pl
from jax.experimental.pallas import tpu as pltpu

F32 = jnp.float32
BF16 = jnp.bfloat16
HI = lax.Precision.HIGHEST

D_MODEL = 1024
DEPTH = 2
GROUP_W = 256
N_HEADS = 4
HEAD_DIM = 64
EPS = 1e-6
RG_C = 8.0
SSD_STATE = 128
SSD_XBC = 768
N_EXPERTS = 16
EXPERT_FF = 2048
EC_CAPACITY_FACTOR = 2
RET_DECAY_EXP = (5.0, 5.5)
ROPE_BASE = 10000.0

LANES = 128
SUBLANES = 8
HALO = SUBLANES
MIX_BLOCK = 256
HG_CHUNK = 64
HG_SUB = 16
PROJ_ROWS = 512
EXP_ROWS = 256
EXP_TOKENS = 4096
VMEM_CAP = 64 * 1024 * 1024

COLS_A = 2 * GROUP_W
COLS_B = 5 * GROUP_W
COLS_C = SSD_XBC + GROUP_W + LANES
COLS_D = 4 * GROUP_W


def _sigmoid(x):
    return 1.0 / (1.0 + jnp.exp(-x))


def _silu(x):
    return x * _sigmoid(x)


def _softplus(x):
    return jnp.maximum(x, 0.0) + jnp.log(1.0 + jnp.exp(-jnp.abs(x)))


def _gelu_tanh(x):
    return 0.5 * x * (1.0 + jnp.tanh(math.sqrt(2.0 / math.pi) * (x + 0.044715 * (x * x * x))))


def _dot(a, b, precision=None):
    return jnp.dot(a, b, preferred_element_type=F32, precision=precision)


def _dot_nt(a, b, precision=None):
    return lax.dot_general(a, b, (((1,), (1,)), ((), ())), preferred_element_type=F32,
                           precision=precision)


def _dot_tn(a, b):
    return lax.dot_general(a, b, (((0,), (0,)), ((), ())), preferred_element_type=F32)


def _head_mask(h, width=GROUP_W, head_dim=HEAD_DIM):
    lane = lax.broadcasted_iota(jnp.int32, (1, width), 1)
    return (lane // head_dim == h).astype(F32)


def _block_diag_mask(n=GROUP_W, head_dim=HEAD_DIM):
    r = lax.broadcasted_iota(jnp.int32, (n, n), 0) // head_dim
    c = lax.broadcasted_iota(jnp.int32, (n, n), 1) // head_dim
    return (r == c).astype(F32)


def _tri(n, reverse):
    r = lax.broadcasted_iota(jnp.int32, (n, n), 0)
    c = lax.broadcasted_iota(jnp.int32, (n, n), 1)
    return (r <= c) if reverse else (r >= c)


def _vmem(nbytes):
    return int(min(VMEM_CAP - (2 << 20), max(nbytes, 16 << 20)))


def _inproj_body(has_delta, *refs):
    if has_delta:
        x_ref, d_ref, nw_ref, w_ref, xo_ref, ua_ref, ub_ref, uc_ref, ud_ref = refs
        x = x_ref[...] + d_ref[...]
        xo_ref[...] = x
    else:
        x_ref, nw_ref, w_ref, ua_ref, ub_ref, uc_ref, ud_ref = refs
        x = x_ref[...]
    ms = jnp.mean(x * x, axis=-1, keepdims=True)
    h = (x * lax.rsqrt(ms + EPS) * nw_ref[...]).astype(BF16)
    c0 = 0
    for ref, n in ((ua_ref, COLS_A), (ub_ref, COLS_B), (uc_ref, COLS_C), (ud_ref, COLS_D)):
        ref[...] = _dot(h, w_ref[:, c0:c0 + n])
        c0 += n


def _inproj(x, delta, norm_w, w_pad):
    t = x.shape[0]
    tm = min(PROJ_ROWS, t)
    ncols = COLS_A + COLS_B + COLS_C + COLS_D
    row = lambda i: (i, 0)
    fixed = lambda i: (0, 0)
    xspec = pl.BlockSpec((tm, D_MODEL), row)
    in_specs = [xspec] + ([xspec] if delta is not None else []) + [
        pl.BlockSpec((1, D_MODEL), fixed), pl.BlockSpec((D_MODEL, ncols), fixed)]
    u_shapes = [jax.ShapeDtypeStruct((t, n), F32) for n in (COLS_A, COLS_B, COLS_C, COLS_D)]
    u_specs = [pl.BlockSpec((tm, n), row) for n in (COLS_A, COLS_B, COLS_C, COLS_D)]
    out_shape = ([jax.ShapeDtypeStruct((t, D_MODEL), F32)] if delta is not None else []) + u_shapes
    out_specs = ([xspec] if delta is not None else []) + u_specs
    args = (x,) + ((delta,) if delta is not None else ()) + (norm_w, w_pad)
    vm = 2 * (2 * tm * D_MODEL * 4 * 2 + D_MODEL * ncols * 2 + tm * ncols * 4) + (8 << 20)
    outs = pl.pallas_call(
        functools.partial(_inproj_body, delta is not None),
        grid=(t // tm,), in_specs=in_specs, out_specs=out_specs, out_shape=out_shape,
        compiler_params=pltpu.CompilerParams(dimension_semantics=("arbitrary",),
                                             vmem_limit_bytes=_vmem(vm)),
    )(*args)
    if delta is not None:
        return outs[0], outs[1:]
    return x, outs


def _conv4(x, prev8, next8, w, bias, first, last):
    n = x.shape[0]
    row = lax.broadcasted_iota(jnp.int32, x.shape, 0)
    pz = jnp.where(first, 0.0, prev8)
    nz = jnp.where(last, 0.0, next8)
    xm1 = jnp.where(row == 0, pz[HALO - 1:HALO], pltpu.roll(x, 1, 0))
    xm2 = jnp.where(row == 0, pz[HALO - 2:HALO - 1],
                    jnp.where(row == 1, pz[HALO - 1:HALO], pltpu.roll(x, 2, 0)))
    xp1 = jnp.where(row == n - 1, nz[0:1], pltpu.roll(x, n - 1, 0))
    return w[0:1] * xm2 + w[1:2] * xm1 + w[2:3] * x + w[3:4] * xp1 + bias


def _mixer_call(body, reverse, b, s, row_inputs, halo_inputs, const_inputs, table_inputs,
                out_dtype, scratch, extra_row_inputs=()):
    nblk = s // MIX_BLOCK
    hb = MIX_BLOCK // HALO

    def blk(j):
        return (nblk - 1 - j) if reverse else j

    in_specs, args = [], []
    for arr, cb, w in tuple(row_inputs) + tuple(extra_row_inputs):
        in_specs.append(pl.BlockSpec((None, MIX_BLOCK, w), lambda bi, j, cb=cb: (bi, blk(j), cb)))
        args.append(arr)
    for arr, cb, w in halo_inputs:
        in_specs.append(pl.BlockSpec((None, MIX_BLOCK, w), lambda bi, j, cb=cb: (bi, blk(j), cb)))
        in_specs.append(pl.BlockSpec(
            (None, HALO, w), lambda bi, j, cb=cb: (bi, jnp.maximum(blk(j) * hb - 1, 0), cb)))
        in_specs.append(pl.BlockSpec(
            (None, HALO, w),
            lambda bi, j, cb=cb: (bi, jnp.minimum((blk(j) + 1) * hb, s // HALO - 1), cb)))
        args += [arr, arr, arr]
    for arr in table_inputs:
        in_specs.append(pl.BlockSpec((MIX_BLOCK, arr.shape[1]), lambda bi, j: (blk(j), 0)))
        args.append(arr)
    for arr in const_inputs:
        in_specs.append(pl.BlockSpec(arr.shape, lambda bi, j, nd=arr.ndim: (0,) * nd))
        args.append(arr)
    return pl.pallas_call(
        functools.partial(body, reverse, nblk),
        grid=(b, nblk), in_specs=in_specs,
        out_specs=pl.BlockSpec((None, MIX_BLOCK, GROUP_W), lambda bi, j: (bi, blk(j), 0)),
        out_shape=jax.ShapeDtypeStruct((b, s, GROUP_W), out_dtype),
        scratch_shapes=scratch,
        compiler_params=pltpu.CompilerParams(dimension_semantics=("arbitrary", "arbitrary"),
                                             vmem_limit_bytes=_vmem(40 << 20)),
    )(*args)


def _block_flags(reverse, nblk):
    j = pl.program_id(1)
    jj = (nblk - 1 - j) if reverse else j
    return j == 0, jj == 0, jj == nblk - 1


def _linear_scan(a, b, reverse):
    n = a.shape[0]
    row = lax.broadcasted_iota(jnp.int32, a.shape, 0)
    d = 1
    while d < n:
        shift = (n - d) if reverse else d
        m = (row < n - d) if reverse else (row >= d)
        a_s = pltpu.roll(a, shift, 0)
        b_s = pltpu.roll(b, shift, 0)
        b = jnp.where(m, a * b_s + b, b)
        a = jnp.where(m, a * a_s, a)
        d *= 2
    return a, b


def _rglru_body(reverse, nblk, *refs):
    if reverse:
        x_ref, xp_ref, xn_ref, cw_ref, cb_ref, wg_ref, bg_ref, lam_ref, out_ref, carry = refs
    else:
        (ga_ref, hb_ref, x_ref, xp_ref, xn_ref, cw_ref, cb_ref, wg_ref, bg_ref, lam_ref,
         out_ref, carry) = refs
    start, first, last = _block_flags(reverse, nblk)

    @pl.when(start)
    def _():
        carry[...] = jnp.zeros_like(carry)

    u = _conv4(x_ref[...], xp_ref[...], xn_ref[...], cw_ref[...], cb_ref[...], first, last)
    g = _dot(u.astype(BF16), wg_ref[...]) + bg_ref[...]
    r = _sigmoid(g[:, :GROUP_W])
    i = _sigmoid(g[:, GROUP_W:])
    log_a = -RG_C * r * _softplus(-lam_ref[...])
    a = jnp.exp(log_a)
    inp = jnp.sqrt(1.0 - jnp.exp(2.0 * log_a)) * i * u
    pa, h0 = _linear_scan(a, inp, reverse)
    h = h0 + pa * carry[...]
    carry[...] = h[0:1] if reverse else h[MIX_BLOCK - 1:MIX_BLOCK]
    if reverse:
        out_ref[...] = h
    else:
        out_ref[...] = ((h + hb_ref[...]) * _gelu_tanh(ga_ref[...])).astype(out_ref.dtype)


def _rglru(ua, p, b, s):
    scratch = [pltpu.VMEM((1, GROUP_W), F32)]
    outs = None
    for reverse in (True, False):
        d = 1 if reverse else 0
        consts = (p["rg_conv_w"], p["rg_conv_b"], p["rg_wg"][d], p["rg_bg"][d], p["rg_lam"][d])
        if reverse:
            outs = _mixer_call(_rglru_body, True, b, s, (), ((ua, 0, GROUP_W),), consts, (), F32,
                               scratch)
        else:
            outs = _mixer_call(_rglru_body, False, b, s, ((ua, 1, GROUP_W), (outs, 0, GROUP_W)),
                               ((ua, 0, GROUP_W),), consts, (), BF16, scratch)
    return outs


def _gla_chunk(q, k, v, lf, st, reverse):
    c, nsub = HG_CHUNK, HG_CHUNK // HG_SUB
    cum = _dot(_tri(c, reverse).astype(F32), lf, HI)
    cum_last = cum[0:1] if reverse else cum[c - 1:c]
    krow = lax.broadcasted_iota(jnp.int32, (c, 1), 0)
    scol = lax.broadcasted_iota(jnp.int32, (HG_SUB, GROUP_W), 1) % HEAD_DIM
    trow = lax.broadcasted_iota(jnp.int32, (HG_SUB, GROUP_W), 0)
    masks = [_head_mask(h) for h in range(N_HEADS)]
    order = list(range(nsub - 1, -1, -1)) if reverse else list(range(nsub))
    blocks = [None] * nsub
    for n_done, i in enumerate(order):
        r0 = i * HG_SUB
        if n_done == 0:
            c0 = jnp.zeros((1, GROUP_W), F32)
        else:
            c0 = cum[r0 + HG_SUB:r0 + HG_SUB + 1] if reverse else cum[r0 - 1:r0]
        qs = q[r0:r0 + HG_SUB] * jnp.exp(cum[r0:r0 + HG_SUB] - c0)
        allowed = (krow >= r0) if reverse else (krow < r0 + HG_SUB)
        ks = k * jnp.exp(jnp.where(allowed, c0 - cum, 0.0))
        ks_bd = jnp.concatenate([(ks * m).astype(BF16) for m in masks], axis=0)
        sc = _dot_nt(qs.astype(BF16), ks_bd)
        keep = (scol >= trow + r0) if reverse else (scol <= trow + r0)
        blocks[i] = jnp.where(keep, sc, 0.0)
    scores = jnp.concatenate(blocks, axis=0).astype(BF16)
    v_bd = jnp.concatenate([(v * m).astype(BF16) for m in masks], axis=0)
    y = _dot(scores, v_bd)
    y = y + _dot_nt((q * jnp.exp(cum)).astype(BF16), st.astype(BF16))
    kw = (k * jnp.exp(cum_last - cum)).astype(BF16)
    st = st * jnp.exp(cum_last) + _block_diag_mask() * _dot_tn(v.astype(BF16), kw)
    return y, st


def _hgrn_body(reverse, nblk, *refs, layer):
    if reverse:
        q_ref, f_ref, i_ref, lbl_ref, out_ref, st_ref = refs
    else:
        q_ref, f_ref, i_ref, g_ref, ob_ref, lbl_ref, nw_ref, bd_ref, out_ref, st_ref = refs
    start, _, _ = _block_flags(reverse, nblk)

    @pl.when(start)
    def _():
        st_ref[...] = jnp.zeros_like(st_ref)

    rows = [lbl_ref[r:r + 1, :] for r in range(DEPTH)]
    mx = functools.reduce(jnp.maximum, rows)
    es = [jnp.exp(r - mx) for r in rows]
    tot = functools.reduce(lambda x, y: x + y, es)
    sm = [e / tot for e in es]
    lb = functools.reduce(lambda x, y: x + y, sm[:layer + 1]) - sm[0]

    f = lb + (1.0 - lb) * _sigmoid(f_ref[...])
    lf = jnp.log(f)
    k = 1.0 - f
    q = q_ref[...]
    v = i_ref[...]
    nchunk = MIX_BLOCK // HG_CHUNK
    st = st_ref[...]
    ys = [None] * nchunk
    for c in (range(nchunk - 1, -1, -1) if reverse else range(nchunk)):
        sl = slice(c * HG_CHUNK, (c + 1) * HG_CHUNK)
        ys[c], st = _gla_chunk(q[sl], k[sl], v[sl], lf[sl], st, reverse)
    st_ref[...] = st
    y = jnp.concatenate(ys, axis=0)
    if reverse:
        out_ref[...] = y
    else:
        o = y + ob_ref[...]
        ms = _dot(o * o, bd_ref[...], HI) * (1.0 / HEAD_DIM)
        o = o * lax.rsqrt(ms + EPS) * nw_ref[...]
        out_ref[...] = (o * _silu(g_ref[...])).astype(out_ref.dtype)


def _hgrn(ub, p, layer, b, s):
    scratch = [pltpu.VMEM((GROUP_W, GROUP_W), F32)]
    ob = _mixer_call(functools.partial(_hgrn_body, layer=layer), True, b, s,
                     ((ub, 0, GROUP_W), (ub, 2, GROUP_W), (ub, 3, GROUP_W)), (),
                     (p["hg_lbl"][1],), (), F32, scratch)
    return _mixer_call(functools.partial(_hgrn_body, layer=layer), False, b, s,
                       ((ub, 0, GROUP_W), (ub, 1, GROUP_W), (ub, 3, GROUP_W), (ub, 4, GROUP_W),
                        (ob, 0, GROUP_W)), (),
                       (p["hg_lbl"][0], p["hg_norm_w"], p["bd_ones"]), (), BF16, scratch)


def _ssd_body(reverse, nblk, *refs):
    if reverse:
        (dt_ref, x_ref, xp_ref, xn_ref, cw_ref, cb_ref, ex_ref, dtb_ref, alog_ref,
         out_ref, st_ref) = refs
    else:
        (dt_ref, z_ref, yb_ref, x_ref, xp_ref, xn_ref, cw_ref, cb_ref, ex_ref, dtb_ref, alog_ref,
         dsk_ref, nw_ref, out_ref, st_ref) = refs
    start, first, last = _block_flags(reverse, nblk)

    @pl.when(start)
    def _():
        st_ref[...] = jnp.zeros_like(st_ref)

    n = MIX_BLOCK
    xbc = _silu(_conv4(x_ref[...], xp_ref[...], xn_ref[...], cw_ref[...], cb_ref[...], first, last))
    xs = xbc[:, :GROUP_W]
    bm = xbc[:, GROUP_W:2 * GROUP_W].astype(BF16)
    cm = xbc[:, 2 * GROUP_W:].astype(BF16)
    dt = _softplus(_dot(dt_ref[...], ex_ref[...], HI) + dtb_ref[...])
    la = dt * (-jnp.exp(alog_ref[...]))
    cum = _dot(_tri(n, reverse).astype(F32), la, HI)
    cum_last = cum[0:1] if reverse else cum[n - 1:n]
    cum_t = cum.T
    v = xs * dt
    keep = _tri(n, reverse)
    y = jnp.zeros((n, GROUP_W), F32)
    sc = [_dot_nt(cm[:, g * SSD_STATE:(g + 1) * SSD_STATE],
                  bm[:, g * SSD_STATE:(g + 1) * SSD_STATE]) for g in range(2)]
    for h in range(N_HEADS):
        l0 = h * HEAD_DIM
        seg = cum[:, l0:l0 + 1] - cum_t[l0:l0 + 1, :]
        dec = jnp.where(keep, jnp.exp(jnp.where(keep, seg, 0.0)), 0.0)
        y = y + _dot((sc[h // 2] * dec).astype(BF16), (v * _head_mask(h)).astype(BF16))
    st = st_ref[...]
    inter = jnp.concatenate(
        [_dot(cm[:, g * SSD_STATE:(g + 1) * SSD_STATE],
              st[:, g * SSD_STATE:(g + 1) * SSD_STATE].astype(BF16)) for g in range(2)], axis=1)
    y = y + jnp.exp(cum) * inter
    vw = (v * jnp.exp(cum_last - cum)).astype(BF16)
    upd = jnp.concatenate(
        [_dot_tn(bm[:, g * SSD_STATE:(g + 1) * SSD_STATE],
                 vw[:, g * SSD_STATE:(g + 1) * SSD_STATE]) for g in range(2)], axis=1)
    st_ref[...] = st * jnp.exp(cum_last) + upd
    if reverse:
        out_ref[...] = y
    else:
        y = (y + yb_ref[...] + dsk_ref[...] * xs) * _silu(z_ref[...])
        ms = jnp.mean(y * y, axis=-1, keepdims=True)
        out_ref[...] = (y * lax.rsqrt(ms + EPS) * nw_ref[...]).astype(out_ref.dtype)


def _ssd(uc, p, b, s):
    scratch = [pltpu.VMEM((SSD_STATE, GROUP_W), F32)]
    dt_cb = (SSD_XBC + GROUP_W) // LANES
    z_cb = SSD_XBC // GROUP_W
    outs = None
    for reverse in (True, False):
        d = 1 if reverse else 0
        consts = (p["ssd_conv_w"], p["ssd_conv_b"], p["ssd_expand"][d], p["ssd_dt_bias"][d],
                  p["ssd_a_log"][d])
        if reverse:
            outs = _mixer_call(_ssd_body, True, b, s, ((uc, dt_cb, LANES),),
                               ((uc, 0, SSD_XBC),), consts, (), F32, scratch)
        else:
            outs = _mixer_call(_ssd_body, False, b, s,
                               ((uc, dt_cb, LANES), (uc, z_cb, GROUP_W), (outs, 0, GROUP_W)),
                               ((uc, 0, SSD_XBC),), consts + (p["ssd_d"], p["ssd_norm_w"]), (),
                               BF16, scratch)
    return outs


def _ret_body(reverse, nblk, *refs):
    if reverse:
        q_ref, k_ref, v_ref, cos_ref, sin_ref, out_ref, st_ref = refs
    else:
        q_ref, k_ref, v_ref, g_ref, ob_ref, cos_ref, sin_ref, bd_ref, out_ref, st_ref = refs
    start, _, _ = _block_flags(reverse, nblk)

    @pl.when(start)
    def _():
        st_ref[...] = jnp.zeros_like(st_ref)

    n = MIX_BLOCK
    lane = lax.broadcasted_iota(jnp.int32, (n, GROUP_W), 1)
    low_half = (lane % HEAD_DIM) < HEAD_DIM // 2
    cos = cos_ref[...]
    sin = sin_ref[...]

    def rot(x):
        swapped = jnp.where(low_half, pltpu.roll(x, GROUP_W - HEAD_DIM // 2, 1),
                            pltpu.roll(x, HEAD_DIM // 2, 1))
        return x * cos + swapped * sin

    exp0 = RET_DECAY_EXP[1] if reverse else RET_DECAY_EXP[0]
    log_g = [math.log1p(-2.0 ** (-exp0 - h)) for h in range(N_HEADS)]
    lane_row = lax.broadcasted_iota(jnp.int32, (1, GROUP_W), 1) // HEAD_DIM
    la = functools.reduce(lambda acc, h: jnp.where(lane_row == h, log_g[h], acc),
                          range(N_HEADS), jnp.zeros((1, GROUP_W), F32))
    rowi = lax.broadcasted_iota(jnp.int32, (n, 1), 0)
    steps = ((n - rowi) if reverse else (rowi + 1)).astype(F32)
    cum = steps * la
    cum_last = float(n) * la
    qr = rot(q_ref[...])
    kr = rot(k_ref[...]) * (HEAD_DIM ** -0.5)
    v = v_ref[...]
    kb = kr.astype(BF16)
    r = lax.broadcasted_iota(jnp.int32, (n, n), 0)
    c = lax.broadcasted_iota(jnp.int32, (n, n), 1)
    dist = ((c - r) if reverse else (r - c))
    keep = dist >= 0
    distf = jnp.where(keep, dist, 0).astype(F32)
    y = jnp.zeros((n, GROUP_W), F32)
    for h in range(N_HEADS):
        m = _head_mask(h)
        dec = jnp.where(keep, jnp.exp(distf * log_g[h]), 0.0)
        sc = _dot_nt((qr * m).astype(BF16), kb)
        y = y + _dot((sc * dec).astype(BF16), (v * m).astype(BF16))
    st = st_ref[...]
    y = y + _dot_nt((qr * jnp.exp(cum)).astype(BF16), st.astype(BF16))
    kw = (kr * jnp.exp(cum_last - cum)).astype(BF16)
    st_ref[...] = st * jnp.exp(cum_last) + _block_diag_mask() * _dot_tn(v.astype(BF16), kw)
    if reverse:
        out_ref[...] = y
    else:
        o = y + ob_ref[...]
        ms = _dot(o * o, bd_ref[...], HI) * (1.0 / HEAD_DIM)
        out_ref[...] = (o * lax.rsqrt(ms + EPS) * _silu(g_ref[...])).astype(out_ref.dtype)


def _ret(ud, p, b, s):
    scratch = [pltpu.VMEM((GROUP_W, GROUP_W), F32)]
    cos, sin = p["rope"][s]
    ob = _mixer_call(_ret_body, True, b, s,
                     ((ud, 0, GROUP_W), (ud, 1, GROUP_W), (ud, 2, GROUP_W)), (), (), (cos, sin),
                     F32, scratch)
    return _mixer_call(_ret_body, False, b, s,
                       ((ud, 0, GROUP_W), (ud, 1, GROUP_W), (ud, 2, GROUP_W), (ud, 3, GROUP_W),
                        (ob, 0, GROUP_W)), (), (p["bd_ones"],), (cos, sin), BF16, scratch)


def _outproj_body(oa_ref, ob_ref, oc_ref, od_ref, x_ref, w_ref, nw_ref, rw_ref,
                  xo_ref, h_ref, aff_ref):
    acc = x_ref[...]
    for n, ref in enumerate((oa_ref, ob_ref, oc_ref, od_ref)):
        acc = acc + _dot(ref[...], w_ref[n * GROUP_W:(n + 1) * GROUP_W, :])
    xo_ref[...] = acc
    ms = jnp.mean(acc * acc, axis=-1, keepdims=True)
    h = acc * lax.rsqrt(ms + EPS) * nw_ref[...]
    h_ref[...] = h.astype(BF16)
    logits = _dot_nt(rw_ref[...], h, HI)
    e = jnp.exp(logits - jnp.max(logits, axis=0, keepdims=True))
    aff_ref[...] = e / jnp.sum(e, axis=0, keepdims=True)


def _outproj(mix, x, w_out, norm_w, router_wt):
    t = x.shape[0]
    tm = min(PROJ_ROWS, t)
    row = lambda i: (i, 0)
    fixed = lambda i: (0, 0)
    in_specs = [pl.BlockSpec((tm, GROUP_W), row)] * 4 + [
        pl.BlockSpec((tm, D_MODEL), row), pl.BlockSpec((D_MODEL, D_MODEL), fixed),
        pl.BlockSpec((1, D_MODEL), fixed), pl.BlockSpec((N_EXPERTS, D_MODEL), fixed)]
    out_specs = [pl.BlockSpec((tm, D_MODEL), row), pl.BlockSpec((tm, D_MODEL), row),
                 pl.BlockSpec((N_EXPERTS, tm), lambda i: (0, i))]
    out_shape = [jax.ShapeDtypeStruct((t, D_MODEL), F32), jax.ShapeDtypeStruct((t, D_MODEL), BF16),
                 jax.ShapeDtypeStruct((N_EXPERTS, t), F32)]
    return pl.pallas_call(
        _outproj_body, grid=(t // tm,), in_specs=in_specs, out_specs=out_specs,
        out_shape=out_shape,
        compiler_params=pltpu.CompilerParams(dimension_semantics=("arbitrary",),
                                             vmem_limit_bytes=_vmem(40 << 20)),
    )(*mix, x, w_out, norm_w, router_wt)


def _route_body(cap, aff_ref, pos_ref, off_ref):
    a = aff_ref[...]
    ng = a.shape[0]
    bits = pltpu.bitcast(a, jnp.int32)

    def count(m):
        return jnp.sum(jnp.sum(m.astype(F32), axis=0, keepdims=True), axis=1, keepdims=True)

    def step(k, prefix):
        cand = prefix | jnp.left_shift(jnp.int32(1), 30 - k)
        return jnp.where(count(bits >= cand) >= cap, cand, prefix)

    thr = lax.fori_loop(0, 31, step, jnp.zeros((1, 1), jnp.int32))
    r = lax.broadcasted_iota(jnp.int32, (LANES, LANES), 0)
    c = lax.broadcasted_iota(jnp.int32, (LANES, LANES), 1)
    incl = (r <= c).astype(BF16)
    ones = jnp.ones((LANES, LANES), BF16)
    gr = lax.broadcasted_iota(jnp.int32, (ng, ng), 0)
    gc = lax.broadcasted_iota(jnp.int32, (ng, ng), 1)
    before = (gc < gr).astype(BF16)

    def prefix(m):
        mb = m.astype(BF16)
        group_off = _dot(before, _dot(mb, ones).astype(BF16))
        return _dot(mb, incl) - m.astype(F32) + group_off, group_off

    gt = bits > thr
    eq = bits == thr
    need = cap - count(gt)
    eq_rank, _ = prefix(eq)
    sel = gt | (eq & (eq_rank < need))
    pos, group_off = prefix(sel)
    pos_ref[...] = jnp.where(sel, pos, -1.0).astype(jnp.int32)
    off_ref[...] = group_off.astype(jnp.int32)


def _route(aff3, cap):
    ne, ng, _ = aff3.shape
    spec = pl.BlockSpec((None, ng, LANES), lambda e: (e, 0, 0))
    return pl.pallas_call(
        functools.partial(_route_body, cap), grid=(ne,), in_specs=[spec], out_specs=[spec, spec],
        out_shape=[jax.ShapeDtypeStruct(aff3.shape, jnp.int32)] * 2,
        compiler_params=pltpu.CompilerParams(dimension_semantics=("arbitrary",),
                                             vmem_limit_bytes=_vmem(32 << 20)),
    )(aff3)


def _expert_body(gpt, ngroups, off_ref, h_ref, pos_ref, gate_ref, wg_ref, wu_ref, wd_ref, y_hbm,
                 y_acc, xe_acc, g_acc, sem):
    i = pl.program_id(0)
    e = pl.program_id(1)

    @pl.when(e == 0)
    def _():
        y_acc[...] = jnp.zeros_like(y_acc)

    obase = e * (ngroups + 1) + i * gpt
    base = off_ref[obase]
    cnt = off_ref[obase + gpt] - base
    row = lax.broadcasted_iota(jnp.int32, (EXP_ROWS, LANES), 0)

    def one_hot(g, lo):
        return row == (pos_ref[pl.ds(g, 1), :] - lo)

    def overlaps(g, lo):
        return (off_ref[obase + g + 1] > lo) & (off_ref[obase + g] < lo + EXP_ROWS)

    def sub_tile(s, carry):
        lo = base + s * EXP_ROWS
        xe_acc[...] = jnp.zeros_like(xe_acc)
        g_acc[...] = jnp.zeros_like(g_acc)

        def gather(g, c):
            @pl.when(overlaps(g, lo))
            def _():
                p = one_hot(g, lo)
                rows = h_ref[pl.ds(pl.multiple_of(g * LANES, LANES), LANES), :]
                xe_acc[...] += _dot(p.astype(BF16), rows)
                g_acc[...] += jnp.sum(jnp.where(p, gate_ref[pl.ds(g, 1), :], 0.0), axis=1,
                                      keepdims=True)
            return c

        lax.fori_loop(0, gpt, gather, 0)
        xe = xe_acc[...].astype(BF16)
        he = (_silu(_dot(xe, wg_ref[...])) * _dot(xe, wu_ref[...])).astype(BF16)
        out = (_dot(he, wd_ref[...]) * g_acc[...]).astype(BF16)

        def scatter(g, c):
            @pl.when(overlaps(g, lo))
            def _():
                p = one_hot(g, lo).astype(BF16)
                sl = pl.ds(pl.multiple_of(g * LANES, LANES), LANES)
                y_acc[sl, :] += _dot_tn(p, out)
            return c

        lax.fori_loop(0, gpt, scatter, 0)
        return carry

    lax.fori_loop(0, (cnt + EXP_ROWS - 1) // EXP_ROWS, sub_tile, 0)

    @pl.when(e == N_EXPERTS - 1)
    def _():
        tt = y_acc.shape[0]
        cp = pltpu.make_async_copy(y_acc, y_hbm.at[pl.ds(pl.multiple_of(i * tt, tt), tt), :], sem)
        cp.start()
        cp.wait()


def _experts(h, pos, gate3, offsets, wg, wu, wd):
    t = h.shape[0]
    tt = min(EXP_TOKENS, t)
    gpt = tt // LANES
    ngroups = t // LANES
    grid_spec = pltpu.PrefetchScalarGridSpec(
        num_scalar_prefetch=1, grid=(t // tt, N_EXPERTS),
        in_specs=[
            pl.BlockSpec((tt, D_MODEL), lambda i, e, off: (i, 0), pipeline_mode=pl.Buffered(1)),
            pl.BlockSpec((None, gpt, LANES), lambda i, e, off: (e, i, 0)),
            pl.BlockSpec((None, gpt, LANES), lambda i, e, off: (e, i, 0)),
            pl.BlockSpec((None, D_MODEL, EXPERT_FF), lambda i, e, off: (e, 0, 0)),
            pl.BlockSpec((None, D_MODEL, EXPERT_FF), lambda i, e, off: (e, 0, 0)),
            pl.BlockSpec((None, EXPERT_FF, D_MODEL), lambda i, e, off: (e, 0, 0)),
        ],
        out_specs=pl.BlockSpec(memory_space=pl.ANY),
        scratch_shapes=[pltpu.VMEM((tt, D_MODEL), F32), pltpu.VMEM((EXP_ROWS, D_MODEL), F32),
                        pltpu.VMEM((EXP_ROWS, 1), F32), pltpu.SemaphoreType.DMA(())],
    )
    vm = (tt * D_MODEL * (2 + 4) + 2 * 3 * D_MODEL * EXPERT_FF * 2
          + EXP_ROWS * (3 * EXPERT_FF + 4 * D_MODEL) * 4 + (4 << 20))
    return pl.pallas_call(
        functools.partial(_expert_body, gpt, ngroups), grid_spec=grid_spec,
        out_shape=jax.ShapeDtypeStruct((t, D_MODEL), F32),
        compiler_params=pltpu.CompilerParams(dimension_semantics=("arbitrary", "arbitrary"),
                                             vmem_limit_bytes=_vmem(vm)),
    )(offsets, h, pos, gate3, wg, wu, wd)


def _expert_choice(h, aff_t, wg, wu, wd):
    t = h.shape[0]
    cap = (EC_CAPACITY_FACTOR * t) // N_EXPERTS
    aff3 = aff_t.reshape(N_EXPERTS, t // LANES, LANES)
    pos, off = _route(aff3, cap)
    offsets = jnp.concatenate([off[:, :, 0], jnp.full((N_EXPERTS, 1), cap, jnp.int32)], axis=1)
    return _experts(h, pos, aff3, offsets.reshape(-1), wg, wu, wd)


def _final_body(x_ref, d_ref, nw_ref, o_ref):
    x = x_ref[...] + d_ref[...]
    ms = jnp.mean(x * x, axis=-1, keepdims=True)
    o_ref[...] = x * lax.rsqrt(ms + EPS) * nw_ref[...]


def _final(x, delta, norm_w):
    t = x.shape[0]
    tm = min(PROJ_ROWS, t)
    spec = pl.BlockSpec((tm, D_MODEL), lambda i: (i, 0))
    return pl.pallas_call(
        _final_body, grid=(t // tm,),
        in_specs=[spec, spec, pl.BlockSpec((1, D_MODEL), lambda i: (0, 0))], out_specs=spec,
        out_shape=jax.ShapeDtypeStruct((t, D_MODEL), F32),
        compiler_params=pltpu.CompilerParams(dimension_semantics=("arbitrary",)),
    )(x, delta, norm_w)


def _block_diag(w):
    nb, k, j = w.shape
    out = jnp.zeros((nb * k, nb * j), w.dtype)
    for n in range(nb):
        out = out.at[n * k:(n + 1) * k, n * j:(n + 1) * j].set(w[n])
    return out


def _rope_tables(s):
    half = HEAD_DIM // 2
    inv_freq = ROPE_BASE ** (-jnp.arange(half, dtype=F32) / half)
    ang = jnp.arange(s, dtype=F32)[:, None] * inv_freq[None, :]
    cos, sin = jnp.cos(ang), jnp.sin(ang)
    cos_t = jnp.tile(jnp.concatenate([cos, cos], axis=1), (1, N_HEADS))
    sin_t = jnp.tile(jnp.concatenate([-sin, sin], axis=1), (1, N_HEADS))
    return cos_t, sin_t


def _prepare(l, seqs, norm_mix, w_in, rg_conv_w, rg_conv_b, rg_wa, rg_ba, rg_wx, rg_bx, rg_lambda,
             hg_lb_logits, hg_norm_w, ssd_conv_w, ssd_conv_b, ssd_dt_bias, ssd_a_log, ssd_d,
             ssd_norm_w, w_out, norm_ffn, router_w, exp_w_gate, exp_w_up, exp_w_down):
    w = w_in[l]
    a_end = COLS_A
    b_end = a_end + COLS_B
    z0 = b_end
    x0 = z0 + GROUP_W
    dt0 = x0 + SSD_XBC
    d0 = dt0 + 2 * N_HEADS
    w_pad = jnp.concatenate(
        [w[:, :b_end], w[:, x0:dt0], w[:, z0:x0], w[:, dt0:d0],
         jnp.zeros((D_MODEL, LANES - 2 * N_HEADS), w.dtype), w[:, d0:]], axis=1).astype(BF16)
    row = lambda v: v.reshape(1, -1).astype(F32)
    rep = lambda v: jnp.repeat(v, HEAD_DIM).reshape(1, GROUP_W).astype(F32)
    expand = []
    for d in range(2):
        ex = np.zeros((LANES, GROUP_W), np.float32)
        for h in range(N_HEADS):
            ex[d * N_HEADS + h, h * HEAD_DIM:(h + 1) * HEAD_DIM] = 1.0
        expand.append(jnp.asarray(ex))
    bd_ones = np.kron(np.eye(N_HEADS, dtype=np.float32), np.ones((HEAD_DIM, HEAD_DIM), np.float32))
    return {
        "norm_mix": row(norm_mix[l]), "w_pad": w_pad,
        "rg_conv_w": rg_conv_w[l], "rg_conv_b": row(rg_conv_b[l]),
        "rg_wg": [jnp.concatenate([_block_diag(rg_wa[l, d]), _block_diag(rg_wx[l, d])],
                                  axis=1).astype(BF16) for d in range(2)],
        "rg_bg": [jnp.concatenate([rg_ba[l, d], rg_bx[l, d]]).reshape(1, -1) for d in range(2)],
        "rg_lam": [row(rg_lambda[l, d]) for d in range(2)],
        "hg_lbl": [hg_lb_logits[:, d, :] for d in range(2)], "hg_norm_w": row(hg_norm_w[l]),
        "bd_ones": jnp.asarray(bd_ones),
        "ssd_conv_w": ssd_conv_w[l], "ssd_conv_b": row(ssd_conv_b[l]), "ssd_expand": expand,
        "ssd_dt_bias": [rep(ssd_dt_bias[l, d]) for d in range(2)],
        "ssd_a_log": [rep(ssd_a_log[l, d]) for d in range(2)],
        "ssd_d": rep(ssd_d[l]), "ssd_norm_w": row(ssd_norm_w[l]),
        "rope": {s: _rope_tables(s) for s in seqs},
        "w_out": w_out[l].astype(BF16), "norm_ffn": row(norm_ffn[l]),
        "router_wt": router_w[l].T,
        "wg": exp_w_gate[l].astype(BF16), "wu": exp_w_up[l].astype(BF16),
        "wd": exp_w_down[l].astype(BF16),
    }


def _trunk(x3, layers, norm_final):
    b, s, _ = x3.shape
    t = b * s
    x = x3.reshape(t, D_MODEL)
    delta = None
    for l, p in enumerate(layers):
        x, (ua, ub, uc, ud) = _inproj(x, delta, p["norm_mix"], p["w_pad"])
        shape3 = lambda u: u.reshape(b, s, u.shape[-1])
        mix = (_rglru(shape3(ua), p, b, s), _hgrn(shape3(ub), p, l, b, s),
               _ssd(shape3(uc), p, b, s), _ret(shape3(ud), p, b, s))
        mix = tuple(m.reshape(t, GROUP_W) for m in mix)
        x, h, aff_t = _outproj(mix, x, p["w_out"], p["norm_ffn"], p["router_wt"])
        delta = _expert_choice(h, aff_t, p["wg"], p["wu"], p["wd"])
    return _final(x, delta, norm_final.reshape(1, -1)).reshape(b, s, D_MODEL)


def kernel(x_prompt, x_sample, norm_mix, w_in, rg_conv_w, rg_conv_b, rg_wa, rg_ba, rg_wx, rg_bx, rg_lambda, hg_lb_logits, hg_norm_w, ssd_conv_w, ssd_conv_b, ssd_dt_bias, ssd_a_log, ssd_d, ssd_norm_w, w_out, norm_ffn, router_w, exp_w_gate, exp_w_up, exp_w_down, norm_final):
    seqs = {x_prompt.shape[1], x_sample.shape[1]}
    layers = [_prepare(l, seqs, norm_mix, w_in, rg_conv_w, rg_conv_b, rg_wa, rg_ba, rg_wx, rg_bx,
                       rg_lambda, hg_lb_logits, hg_norm_w, ssd_conv_w, ssd_conv_b, ssd_dt_bias,
                       ssd_a_log, ssd_d, ssd_norm_w, w_out, norm_ffn, router_w, exp_w_gate,
                       exp_w_up, exp_w_down) for l in range(DEPTH)]
    return (_trunk(x_prompt, layers, norm_final), _trunk(x_sample, layers, norm_final))
```

```python
import functools
import math

import jax
import jax.numpy as jnp
import numpy as np
from jax import lax
from jax.experimental import pallas as pl
from jax.experimental.pallas import tpu as pltpu

F32 = jnp.float32
BF16 = jnp.bfloat16
HI = lax.Precision.HIGHEST

D_MODEL = 1024
DEPTH = 2
GROUP_W = 256
N_HEADS = 4
HEAD_DIM = 64
EPS = 1e-6
RG_C = 8.0
SSD_STATE = 128
SSD_XBC = 768
N_EXPERTS = 16
EXPERT_FF = 2048
EC_CAPACITY_FACTOR = 2
RET_DECAY_EXP = (5.0, 5.5)
ROPE_BASE = 10000.0

LANES = 128
SUBLANES = 8
HALO = SUBLANES
MIX_BLOCK = 256
MIX_BATCH = 2
HG_CHUNK = 64
HG_SUB = 16
PROJ_ROWS = 512
EXP_ROWS = 256
EXP_TOKENS = 4096
VMEM_CAP = 64 * 1024 * 1024

COLS_A = 2 * GROUP_W
COLS_B = 5 * GROUP_W
COLS_C = SSD_XBC + GROUP_W + LANES
COLS_D = 4 * GROUP_W


def _sigmoid(x):
    return 1.0 / (1.0 + jnp.exp(-x))


def _silu(x):
    return x * _sigmoid(x)


def _softplus(x):
    return jnp.maximum(x, 0.0) + jnp.log(1.0 + jnp.exp(-jnp.abs(x)))


def _gelu_tanh(x):
    return 0.5 * x * (1.0 + jnp.tanh(math.sqrt(2.0 / math.pi) * (x + 0.044715 * (x * x * x))))


def _dot(a, b, precision=None):
    return jnp.dot(a, b, preferred_element_type=F32, precision=precision)


def _dot_nt(a, b, precision=None):
    return lax.dot_general(a, b, (((1,), (1,)), ((), ())), preferred_element_type=F32,
                           precision=precision)


def _dot_tn(a, b):
    return lax.dot_general(a, b, (((0,), (0,)), ((), ())), preferred_element_type=F32)


def _head_mask(h, width=GROUP_W, head_dim=HEAD_DIM):
    lane = lax.broadcasted_iota(jnp.int32, (1, width), 1)
    return (lane // head_dim == h).astype(F32)


def _tri(n, reverse):
    r = lax.broadcasted_iota(jnp.int32, (n, n), 0)
    c = lax.broadcasted_iota(jnp.int32, (n, n), 1)
    return (r <= c) if reverse else (r >= c)


def _vmem(nbytes):
    return int(min(VMEM_CAP - (2 << 20), max(nbytes, 16 << 20)))


def _inproj_body(has_delta, *refs):
    if has_delta:
        x_ref, d_ref, nw_ref, w_ref, xo_ref, ua_ref, ub_ref, uc_ref, ud_ref = refs
        x = x_ref[...] + d_ref[...]
        xo_ref[...] = x
    else:
        x_ref, nw_ref, w_ref, ua_ref, ub_ref, uc_ref, ud_ref = refs
        x = x_ref[...]
    ms = jnp.mean(x * x, axis=-1, keepdims=True)
    h = (x * lax.rsqrt(ms + EPS) * nw_ref[...]).astype(BF16)
    c0 = 0
    for ref, n in ((ua_ref, COLS_A), (ub_ref, COLS_B), (uc_ref, COLS_C), (ud_ref, COLS_D)):
        ref[...] = _dot(h, w_ref[:, c0:c0 + n])
        c0 += n


def _inproj(x, delta, norm_w, w_pad):
    t = x.shape[0]
    tm = min(PROJ_ROWS, t)
    ncols = COLS_A + COLS_B + COLS_C + COLS_D
    row = lambda i: (i, 0)
    fixed = lambda i: (0, 0)
    xspec = pl.BlockSpec((tm, D_MODEL), row)
    in_specs = [xspec] + ([xspec] if delta is not None else []) + [
        pl.BlockSpec((1, D_MODEL), fixed), pl.BlockSpec((D_MODEL, ncols), fixed)]
    u_shapes = [jax.ShapeDtypeStruct((t, n), F32) for n in (COLS_A, COLS_B, COLS_C, COLS_D)]
    u_specs = [pl.BlockSpec((tm, n), row) for n in (COLS_A, COLS_B, COLS_C, COLS_D)]
    out_shape = ([jax.ShapeDtypeStruct((t, D_MODEL), F32)] if delta is not None else []) + u_shapes
    out_specs = ([xspec] if delta is not None else []) + u_specs
    args = (x,) + ((delta,) if delta is not None else ()) + (norm_w, w_pad)
    vm = 2 * (2 * tm * D_MODEL * 4 * 2 + D_MODEL * ncols * 2 + tm * ncols * 4) + (8 << 20)
    outs = pl.pallas_call(
        functools.partial(_inproj_body, delta is not None),
        grid=(t // tm,), in_specs=in_specs, out_specs=out_specs, out_shape=out_shape,
        compiler_params=pltpu.CompilerParams(dimension_semantics=("arbitrary",),
                                             vmem_limit_bytes=_vmem(vm)),
    )(*args)
    if delta is not None:
        return outs[0], outs[1:]
    return x, outs


def _conv4(x, prev8, next8, w, bias, first, last):
    n = x.shape[0]
    row = lax.broadcasted_iota(jnp.int32, x.shape, 0)
    pz = jnp.where(first, 0.0, prev8)
    nz = jnp.where(last, 0.0, next8)
    xm1 = jnp.where(row == 0, pz[HALO - 1:HALO], pltpu.roll(x, 1, 0))
    xm2 = jnp.where(row == 0, pz[HALO - 2:HALO - 1],
                    jnp.where(row == 1, pz[HALO - 1:HALO], pltpu.roll(x, 2, 0)))
    xp1 = jnp.where(row == n - 1, nz[0:1], pltpu.roll(x, n - 1, 0))
    return w[0:1] * xm2 + w[1:2] * xm1 + w[2:3] * x + w[3:4] * xp1 + bias


def _mixer_call(body, reverse, b, s, row_inputs, halo_inputs, const_inputs, table_inputs,
                out_dtype, state, tables=(), table_init=None, extra_row_inputs=()):
    nblk = s // MIX_BLOCK
    hb = MIX_BLOCK // HALO
    nb = MIX_BATCH if b % MIX_BATCH == 0 else 1

    def blk(j):
        return (nblk - 1 - j) if reverse else j

    in_specs, args, per_batch = [], [], []
    for arr, cb, w in tuple(row_inputs) + tuple(extra_row_inputs):
        in_specs.append(pl.BlockSpec((nb, MIX_BLOCK, w), lambda bi, j, cb=cb: (bi, blk(j), cb)))
        args.append(arr)
    for arr, cb, w in halo_inputs:
        in_specs.append(pl.BlockSpec((nb, MIX_BLOCK, w), lambda bi, j, cb=cb: (bi, blk(j), cb)))
        in_specs.append(pl.BlockSpec(
            (nb, HALO, w), lambda bi, j, cb=cb: (bi, jnp.maximum(blk(j) * hb - 1, 0), cb)))
        in_specs.append(pl.BlockSpec(
            (nb, HALO, w),
            lambda bi, j, cb=cb: (bi, jnp.minimum((blk(j) + 1) * hb, s // HALO - 1), cb)))
        args += [arr, arr, arr]
    per_batch += [True] * len(args)
    for arr in table_inputs:
        in_specs.append(pl.BlockSpec((MIX_BLOCK, arr.shape[1]), lambda bi, j: (blk(j), 0)))
        args.append(arr)
    for arr in const_inputs:
        in_specs.append(pl.BlockSpec(arr.shape, lambda bi, j, nd=arr.ndim: (0,) * nd))
        args.append(arr)
    per_batch += [False] * (len(args) - len(per_batch)) + [True] + [True] * len(state)
    per_batch += [False] * len(tables)

    def step(*refs):
        scratch_refs = refs[len(refs) - len(state) - len(tables):]

        @pl.when(pl.program_id(1) == 0)
        def _():
            for r in scratch_refs[:len(state)]:
                r[...] = jnp.zeros_like(r)

        if table_init is not None:
            @pl.when((pl.program_id(0) == 0) & (pl.program_id(1) == 0))
            def _():
                table_init(reverse, *scratch_refs[len(state):])

        for n in range(nb):
            body(reverse, nblk, *[r.at[n] if pb else r for r, pb in zip(refs, per_batch)])

    return pl.pallas_call(
        step, grid=(b // nb, nblk), in_specs=in_specs,
        out_specs=pl.BlockSpec((nb, MIX_BLOCK, GROUP_W), lambda bi, j: (bi, blk(j), 0)),
        out_shape=jax.ShapeDtypeStruct((b, s, GROUP_W), out_dtype),
        scratch_shapes=([pltpu.VMEM((nb,) + shape, dtype) for shape, dtype in state]
                        + [pltpu.VMEM(shape, dtype) for shape, dtype in tables]),
        compiler_params=pltpu.CompilerParams(dimension_semantics=("arbitrary", "arbitrary"),
                                             vmem_limit_bytes=_vmem(40 << 20)),
    )(*args)


def _edge_flags(reverse, nblk):
    j = pl.program_id(1)
    jj = (nblk - 1 - j) if reverse else j
    return jj == 0, jj == nblk - 1


def _linear_scan(a, b, reverse):
    n = a.shape[0]
    row = lax.broadcasted_iota(jnp.int32, a.shape, 0)
    d = 1
    while d < n:
        shift = (n - d) if reverse else d
        m = (row < n - d) if reverse else (row >= d)
        a_s = pltpu.roll(a, shift, 0)
        b_s = pltpu.roll(b, shift, 0)
        b = jnp.where(m, a * b_s + b, b)
        a = jnp.where(m, a * a_s, a)
        d *= 2
    return a, b


def _rglru_body(reverse, nblk, *refs):
    if reverse:
        x_ref, xp_ref, xn_ref, cw_ref, cb_ref, wg_ref, bg_ref, lam_ref, out_ref, carry = refs
    else:
        (ga_ref, hb_ref, x_ref, xp_ref, xn_ref, cw_ref, cb_ref, wg_ref, bg_ref, lam_ref,
         out_ref, carry) = refs
    first, last = _edge_flags(reverse, nblk)
    u = _conv4(x_ref[...], xp_ref[...], xn_ref[...], cw_ref[...], cb_ref[...], first, last)
    g = _dot(u.astype(BF16), wg_ref[...]) + bg_ref[...]
    r = _sigmoid(g[:, :GROUP_W])
    i = _sigmoid(g[:, GROUP_W:])
    log_a = -RG_C * r * _softplus(-lam_ref[...])
    a = jnp.exp(log_a)
    inp = jnp.sqrt(1.0 - jnp.exp(2.0 * log_a)) * i * u
    pa, h0 = _linear_scan(a, inp, reverse)
    h = h0 + pa * carry[...]
    carry[...] = h[0:1] if reverse else h[MIX_BLOCK - 1:MIX_BLOCK]
    if reverse:
        out_ref[...] = h
    else:
        out_ref[...] = ((h + hb_ref[...]) * _gelu_tanh(ga_ref[...])).astype(out_ref.dtype)


def _rglru(ua, p, b, s):
    scratch = [((1, GROUP_W), F32)]
    outs = None
    for reverse in (True, False):
        d = 1 if reverse else 0
        consts = (p["rg_conv_w"], p["rg_conv_b"], p["rg_wg"][d], p["rg_bg"][d], p["rg_lam"][d])
        if reverse:
            outs = _mixer_call(_rglru_body, True, b, s, (), ((ua, 0, GROUP_W),), consts, (), F32,
                               scratch)
        else:
            outs = _mixer_call(_rglru_body, False, b, s, ((ua, 1, GROUP_W), (outs, 0, GROUP_W)),
                               ((ua, 0, GROUP_W),), consts, (), BF16, scratch)
    return outs


def _split3(x):
    x0 = x.astype(BF16)
    r1 = x - x0.astype(F32)
    x1 = r1.astype(BF16)
    return x0, x1, (r1 - x1.astype(F32)).astype(BF16)


def _cumsum_rows(tri, x):
    return functools.reduce(lambda a, b: a + b, [_dot(tri, piece) for piece in _split3(x)])


def _select_cols(x, sel):
    return functools.reduce(lambda a, b: a + b, [_dot(piece, sel) for piece in _split3(x)])


def _head_mean_sq(o, bd):
    sq = o * o
    hi = sq.astype(BF16)
    lo = (sq - hi.astype(F32)).astype(BF16)
    return (_dot(hi, bd) + _dot(lo, bd)) * (1.0 / HEAD_DIM)


def _gla_chunk(q, k, v, cum, st, bd, reverse):
    c, nsub = HG_CHUNK, HG_CHUNK // HG_SUB
    cum_last = cum[0:1] if reverse else cum[c - 1:c]
    masks = [_head_mask(h) for h in range(N_HEADS)]
    order = list(range(nsub - 1, -1, -1)) if reverse else list(range(nsub))
    entry = {}
    for n_done, i in enumerate(order):
        r0 = i * HG_SUB
        if n_done == 0:
            entry[i] = jnp.zeros((1, GROUP_W), F32)
        else:
            entry[i] = cum[r0 + HG_SUB:r0 + HG_SUB + 1] if reverse else cum[r0 - 1:r0]
    entry_rows = jnp.concatenate(
        [jnp.broadcast_to(entry[i], (HG_SUB, GROUP_W)) for i in range(nsub)], axis=0)
    qh = q * jnp.exp(cum - entry_rows)
    kh = k * jnp.exp(entry_rows - cum)
    kh_bd = jnp.concatenate([(kh * m).astype(BF16) for m in masks], axis=0)
    pairs, lhs = [], []
    for n_done, i in enumerate(order):
        qi = qh[i * HG_SUB:(i + 1) * HG_SUB]
        for j in order[:n_done + 1]:
            pairs.append((i, j))
            lhs.append((qi if j == i else qi * jnp.exp(entry[i] - entry[j])).astype(BF16))
    res = _dot_nt(jnp.concatenate(lhs, axis=0), kh_bd)
    scol = lax.broadcasted_iota(jnp.int32, (HG_SUB, GROUP_W), 1) % HEAD_DIM
    trow = lax.broadcasted_iota(jnp.int32, (HG_SUB, GROUP_W), 0)
    blocks = [None] * nsub
    for n, (i, j) in enumerate(pairs):
        keep = scol // HG_SUB == j
        if i == j:
            keep = keep & ((scol >= trow + i * HG_SUB) if reverse else (scol <= trow + i * HG_SUB))
        piece = jnp.where(keep, res[n * HG_SUB:(n + 1) * HG_SUB], 0.0)
        blocks[i] = piece if blocks[i] is None else blocks[i] + piece
    scores = jnp.concatenate(blocks, axis=0).astype(BF16)
    v_bd = jnp.concatenate([(v * m).astype(BF16) for m in masks], axis=0)
    y = _dot(scores, v_bd)
    y = y + _dot_nt((q * jnp.exp(cum)).astype(BF16), st.astype(BF16))
    kw = (k * jnp.exp(cum_last - cum)).astype(BF16)
    st = st * jnp.exp(cum_last) + bd * _dot_tn(v.astype(BF16), kw)
    return y, st


def _hgrn_body(reverse, nblk, *refs, layer):
    if reverse:
        q_ref, f_ref, i_ref, lbl_ref, tri_ref, bd_ref, out_ref, st_ref = refs
    else:
        (q_ref, f_ref, i_ref, g_ref, ob_ref, lbl_ref, tri_ref, bd_ref, nw_ref, bdb_ref, out_ref,
         st_ref) = refs
    rows = [lbl_ref[r:r + 1, :] for r in range(DEPTH)]
    mx = functools.reduce(jnp.maximum, rows)
    es = [jnp.exp(r - mx) for r in rows]
    tot = functools.reduce(lambda x, y: x + y, es)
    sm = [e / tot for e in es]
    lb = functools.reduce(lambda x, y: x + y, sm[:layer + 1]) - sm[0]

    f = lb + (1.0 - lb) * _sigmoid(f_ref[...])
    lf = jnp.log(f)
    k = 1.0 - f
    q = q_ref[...]
    v = i_ref[...]
    nchunk = MIX_BLOCK // HG_CHUNK
    cum = _cumsum_rows(tri_ref[...], lf)
    bd = bd_ref[...]
    st = st_ref[...]
    ys = [None] * nchunk
    for c in (range(nchunk - 1, -1, -1) if reverse else range(nchunk)):
        sl = slice(c * HG_CHUNK, (c + 1) * HG_CHUNK)
        ys[c], st = _gla_chunk(q[sl], k[sl], v[sl], cum[sl], st, bd, reverse)
    st_ref[...] = st
    y = jnp.concatenate(ys, axis=0)
    if reverse:
        out_ref[...] = y
    else:
        o = y + ob_ref[...]
        o = o * lax.rsqrt(_head_mean_sq(o, bdb_ref[...]) + EPS) * nw_ref[...]
        out_ref[...] = (o * _silu(g_ref[...])).astype(out_ref.dtype)


def _hgrn(ub, p, layer, b, s):
    scratch = [((GROUP_W, GROUP_W), F32)]
    ob = _mixer_call(functools.partial(_hgrn_body, layer=layer), True, b, s,
                     ((ub, 0, GROUP_W), (ub, 2, GROUP_W), (ub, 3, GROUP_W)), (),
                     (p["hg_lbl"][1], p["tri_chunk"][1], p["bd_f32"]), (), F32, scratch)
    return _mixer_call(functools.partial(_hgrn_body, layer=layer), False, b, s,
                       ((ub, 0, GROUP_W), (ub, 1, GROUP_W), (ub, 3, GROUP_W), (ub, 4, GROUP_W),
                        (ob, 0, GROUP_W)), (),
                       (p["hg_lbl"][0], p["tri_chunk"][0], p["bd_f32"], p["hg_norm_w"],
                        p["bd_bf16"]), (), BF16, scratch)


def _ssd_body(reverse, nblk, *refs):
    if reverse:
        (dt_ref, x_ref, xp_ref, xn_ref, cw_ref, cb_ref, ex_ref, dtb_ref, alog_ref, tri_ref,
         out_ref, st_ref) = refs
    else:
        (dt_ref, z_ref, yb_ref, x_ref, xp_ref, xn_ref, cw_ref, cb_ref, ex_ref, dtb_ref, alog_ref,
         tri_ref, dsk_ref, nw_ref, out_ref, st_ref) = refs
    first, last = _edge_flags(reverse, nblk)
    n = MIX_BLOCK
    xbc = _silu(_conv4(x_ref[...], xp_ref[...], xn_ref[...], cw_ref[...], cb_ref[...], first, last))
    xs = xbc[:, :GROUP_W]
    bm = xbc[:, GROUP_W:2 * GROUP_W].astype(BF16)
    cm = xbc[:, 2 * GROUP_W:].astype(BF16)
    dt = _softplus(_select_cols(dt_ref[...], ex_ref[...]) + dtb_ref[...])
    la = dt * (-jnp.exp(alog_ref[...]))
    cum = _cumsum_rows(tri_ref[...], la)
    cum_last = cum[0:1] if reverse else cum[n - 1:n]
    cum_t = cum.T
    v = xs * dt
    keep = _tri(n, reverse)
    y = jnp.zeros((n, GROUP_W), F32)
    sc = [_dot_nt(cm[:, g * SSD_STATE:(g + 1) * SSD_STATE],
                  bm[:, g * SSD_STATE:(g + 1) * SSD_STATE]) for g in range(2)]
    for h in range(N_HEADS):
        l0 = h * HEAD_DIM
        seg = cum[:, l0:l0 + 1] - cum_t[l0:l0 + 1, :]
        dec = jnp.where(keep, jnp.exp(jnp.where(keep, seg, 0.0)), 0.0)
        y = y + _dot((sc[h // 2] * dec).astype(BF16), (v * _head_mask(h)).astype(BF16))
    st = st_ref[...]
    inter = jnp.concatenate(
        [_dot(cm[:, g * SSD_STATE:(g + 1) * SSD_STATE],
              st[:, g * SSD_STATE:(g + 1) * SSD_STATE].astype(BF16)) for g in range(2)], axis=1)
    y = y + jnp.exp(cum) * inter
    vw = (v * jnp.exp(cum_last - cum)).astype(BF16)
    upd = jnp.concatenate(
        [_dot_tn(bm[:, g * SSD_STATE:(g + 1) * SSD_STATE],
                 vw[:, g * SSD_STATE:(g + 1) * SSD_STATE]) for g in range(2)], axis=1)
    st_ref[...] = st * jnp.exp(cum_last) + upd
    if reverse:
        out_ref[...] = y
    else:
        y = (y + yb_ref[...] + dsk_ref[...] * xs) * _silu(z_ref[...])
        ms = jnp.mean(y * y, axis=-1, keepdims=True)
        out_ref[...] = (y * lax.rsqrt(ms + EPS) * nw_ref[...]).astype(out_ref.dtype)


def _ssd(uc, p, b, s):
    scratch = [((SSD_STATE, GROUP_W), F32)]
    dt_cb = (SSD_XBC + GROUP_W) // LANES
    z_cb = SSD_XBC // GROUP_W
    outs = None
    for reverse in (True, False):
        d = 1 if reverse else 0
        consts = (p["ssd_conv_w"], p["ssd_conv_b"], p["ssd_expand"][d], p["ssd_dt_bias"][d],
                  p["ssd_a_log"][d], p["tri_block"][d])
        if reverse:
            outs = _mixer_call(_ssd_body, True, b, s, ((uc, dt_cb, LANES),),
                               ((uc, 0, SSD_XBC),), consts, (), F32, scratch)
        else:
            outs = _mixer_call(_ssd_body, False, b, s,
                               ((uc, dt_cb, LANES), (uc, z_cb, GROUP_W), (outs, 0, GROUP_W)),
                               ((uc, 0, SSD_XBC),), consts + (p["ssd_d"], p["ssd_norm_w"]), (),
                               BF16, scratch)
    return outs


def _ret_log_decay(reverse):
    exp0 = RET_DECAY_EXP[1] if reverse else RET_DECAY_EXP[0]
    return [math.log1p(-2.0 ** (-exp0 - h)) for h in range(N_HEADS)]


def _ret_decay_table(reverse, dec_ref):
    n = MIX_BLOCK
    r = lax.broadcasted_iota(jnp.int32, (n, n), 0)
    c = lax.broadcasted_iota(jnp.int32, (n, n), 1)
    dist = (c - r) if reverse else (r - c)
    keep = dist >= 0
    distf = jnp.where(keep, dist, 0).astype(F32)
    for h, lg in enumerate(_ret_log_decay(reverse)):
        dec_ref[h] = jnp.where(keep, jnp.exp(distf * lg), 0.0)


def _ret_body(reverse, nblk, *refs):
    if reverse:
        q_ref, k_ref, v_ref, cos_ref, sin_ref, bd_ref, out_ref, st_ref, dec_ref = refs
    else:
        (q_ref, k_ref, v_ref, g_ref, ob_ref, cos_ref, sin_ref, bd_ref, bdb_ref, out_ref, st_ref,
         dec_ref) = refs
    n = MIX_BLOCK
    log_g = _ret_log_decay(reverse)
    lane = lax.broadcasted_iota(jnp.int32, (n, GROUP_W), 1)
    low_half = (lane % HEAD_DIM) < HEAD_DIM // 2
    cos = cos_ref[...]
    sin = sin_ref[...]

    def rot(x):
        swapped = jnp.where(low_half, pltpu.roll(x, GROUP_W - HEAD_DIM // 2, 1),
                            pltpu.roll(x, HEAD_DIM // 2, 1))
        return x * cos + swapped * sin

    lane_row = lax.broadcasted_iota(jnp.int32, (1, GROUP_W), 1) // HEAD_DIM
    la = functools.reduce(lambda acc, h: jnp.where(lane_row == h, log_g[h], acc),
                          range(N_HEADS), jnp.zeros((1, GROUP_W), F32))
    rowi = lax.broadcasted_iota(jnp.int32, (n, 1), 0)
    steps = ((n - rowi) if reverse else (rowi + 1)).astype(F32)
    cum = steps * la
    cum_last = float(n) * la
    qr = rot(q_ref[...])
    kr = rot(k_ref[...]) * (HEAD_DIM ** -0.5)
    v = v_ref[...]
    kb = kr.astype(BF16)
    y = jnp.zeros((n, GROUP_W), F32)
    for h in range(N_HEADS):
        m = _head_mask(h)
        sc = _dot_nt((qr * m).astype(BF16), kb)
        y = y + _dot((sc * dec_ref[h]).astype(BF16), (v * m).astype(BF16))
    st = st_ref[...]
    y = y + _dot_nt((qr * jnp.exp(cum)).astype(BF16), st.astype(BF16))
    kw = (kr * jnp.exp(cum_last - cum)).astype(BF16)
    st_ref[...] = st * jnp.exp(cum_last) + bd_ref[...] * _dot_tn(v.astype(BF16), kw)
    if reverse:
        out_ref[...] = y
    else:
        o = y + ob_ref[...]
        ms = _head_mean_sq(o, bdb_ref[...])
        out_ref[...] = (o * lax.rsqrt(ms + EPS) * _silu(g_ref[...])).astype(out_ref.dtype)


def _ret(ud, p, b, s):
    state = [((GROUP_W, GROUP_W), F32)]
    tables = [((N_HEADS, MIX_BLOCK, MIX_BLOCK), F32)]
    cos, sin = p["rope"][s]
    ob = _mixer_call(_ret_body, True, b, s,
                     ((ud, 0, GROUP_W), (ud, 1, GROUP_W), (ud, 2, GROUP_W)), (), (p["bd_f32"],),
                     (cos, sin), F32, state, tables, _ret_decay_table)
    return _mixer_call(_ret_body, False, b, s,
                       ((ud, 0, GROUP_W), (ud, 1, GROUP_W), (ud, 2, GROUP_W), (ud, 3, GROUP_W),
                        (ob, 0, GROUP_W)), (), (p["bd_f32"], p["bd_bf16"]), (cos, sin), BF16,
                       state, tables, _ret_decay_table)


def _outproj_body(oa_ref, ob_ref, oc_ref, od_ref, x_ref, w_ref, nw_ref, rw_ref,
                  xo_ref, h_ref, aff_ref):
    acc = x_ref[...]
    for n, ref in enumerate((oa_ref, ob_ref, oc_ref, od_ref)):
        acc = acc + _dot(ref[...], w_ref[n * GROUP_W:(n + 1) * GROUP_W, :])
    xo_ref[...] = acc
    ms = jnp.mean(acc * acc, axis=-1, keepdims=True)
    h = acc * lax.rsqrt(ms + EPS) * nw_ref[...]
    bits = pltpu.bitcast(h.astype(BF16).astype(F32), jnp.uint32)
    half = D_MODEL // 2
    h_ref[...] = (bits[:, :half] >> 16) | (bits[:, half:] & jnp.uint32(0xFFFF0000))
    logits = _dot_nt(rw_ref[...], h, HI)
    e = jnp.exp(logits - jnp.max(logits, axis=0, keepdims=True))
    aff_ref[...] = e / jnp.sum(e, axis=0, keepdims=True)


def _outproj(mix, x, w_out, norm_w, router_wt):
    t = x.shape[0]
    tm = min(PROJ_ROWS, t)
    row = lambda i: (i, 0)
    fixed = lambda i: (0, 0)
    in_specs = [pl.BlockSpec((tm, GROUP_W), row)] * 4 + [
        pl.BlockSpec((tm, D_MODEL), row), pl.BlockSpec((D_MODEL, D_MODEL), fixed),
        pl.BlockSpec((1, D_MODEL), fixed), pl.BlockSpec((N_EXPERTS, D_MODEL), fixed)]
    out_specs = [pl.BlockSpec((tm, D_MODEL), row), pl.BlockSpec((tm, D_MODEL // 2), row),
                 pl.BlockSpec((N_EXPERTS, tm), lambda i: (0, i))]
    out_shape = [jax.ShapeDtypeStruct((t, D_MODEL), F32),
                 jax.ShapeDtypeStruct((t, D_MODEL // 2), jnp.uint32),
                 jax.ShapeDtypeStruct((N_EXPERTS, t), F32)]
    return pl.pallas_call(
        _outproj_body, grid=(t // tm,), in_specs=in_specs, out_specs=out_specs,
        out_shape=out_shape,
        compiler_params=pltpu.CompilerParams(dimension_semantics=("arbitrary",),
                                             vmem_limit_bytes=_vmem(40 << 20)),
    )(*mix, x, w_out, norm_w, router_wt)


def _route_body(cap, aff_ref, pos_ref, off_ref):
    a = aff_ref[...]
    ng = a.shape[0]
    bits = pltpu.bitcast(a, jnp.int32)

    def count(m):
        return jnp.sum(jnp.sum(m.astype(F32), axis=0, keepdims=True), axis=1, keepdims=True)

    def step(k, prefix):
        cand = prefix | jnp.left_shift(jnp.int32(1), 30 - k)
        return jnp.where(count(bits >= cand) >= cap, cand, prefix)

    thr = lax.fori_loop(0, 31, step, jnp.zeros((1, 1), jnp.int32))
    r = lax.broadcasted_iota(jnp.int32, (LANES, LANES), 0)
    c = lax.broadcasted_iota(jnp.int32, (LANES, LANES), 1)
    incl = (r <= c).astype(BF16)
    ones = jnp.ones((LANES, LANES), BF16)
    gr = lax.broadcasted_iota(jnp.int32, (ng, ng), 0)
    gc = lax.broadcasted_iota(jnp.int32, (ng, ng), 1)
    before = (gc < gr).astype(BF16)

    def prefix(m):
        mb = m.astype(BF16)
        group_off = _dot(before, _dot(mb, ones).astype(BF16))
        return _dot(mb, incl) - m.astype(F32) + group_off, group_off

    gt = bits > thr
    eq = bits == thr
    need = cap - count(gt)
    eq_rank, _ = prefix(eq)
    sel = gt | (eq & (eq_rank < need))
    pos, group_off = prefix(sel)
    pos_ref[...] = jnp.where(sel, pos, -1.0).astype(jnp.int32)
    off_ref[...] = group_off.astype(jnp.int32)


def _route(aff3, cap):
    ne, ng, _ = aff3.shape
    spec = pl.BlockSpec((None, ng, LANES), lambda e: (e, 0, 0))
    return pl.pallas_call(
        functools.partial(_route_body, cap), grid=(ne,), in_specs=[spec], out_specs=[spec, spec],
        out_shape=[jax.ShapeDtypeStruct(aff3.shape, jnp.int32)] * 2,
        compiler_params=pltpu.CompilerParams(dimension_semantics=("arbitrary",),
                                             vmem_limit_bytes=_vmem(32 << 20)),
    )(aff3)


def _tile_span(off_ref, gpt, ngroups):
    obase = pl.program_id(1) * (ngroups + 1) + pl.program_id(0) * gpt
    base = off_ref[obase]
    return base, off_ref[obase + gpt] - base, obase


def _lists_body(gpt, ngroups, off_ref, pos_ref, idx_ref):
    base, _, obase = _tile_span(off_ref, gpt, ngroups)
    nsub = idx_ref.shape[0]
    pos_t = pos_ref[...].astype(F32).T
    tok = lax.broadcasted_iota(jnp.int32, (LANES, EXP_ROWS), 0).astype(F32)
    row = lax.broadcasted_iota(jnp.int32, (LANES, EXP_ROWS), 1).astype(F32)
    idx_ref[...] = jnp.zeros_like(idx_ref)
    for g in range(gpt):
        first = (off_ref[obase + g] - base) // EXP_ROWS
        rel = pos_t[:, g:g + 1] - base.astype(F32)
        for s in (first, first + 1):
            hit = (rel - (s * EXP_ROWS).astype(F32)) == row
            add = jnp.sum(jnp.where(hit, tok + float(g * LANES), 0.0), axis=0, keepdims=True)
            idx_ref[jnp.minimum(s, nsub - 1)] += add.astype(jnp.int32)


def _lists(pos, offsets, tt):
    ne, ngroups, _ = pos.shape
    gpt = tt // LANES
    ntiles = ngroups // gpt
    nsub = tt // EXP_ROWS
    shape = (ne, ntiles, nsub, 1, EXP_ROWS)
    grid_spec = pltpu.PrefetchScalarGridSpec(
        num_scalar_prefetch=1, grid=(ntiles, ne),
        in_specs=[pl.BlockSpec((None, gpt, LANES), lambda i, e, off: (e, i, 0))],
        out_specs=pl.BlockSpec((None, None, nsub, 1, EXP_ROWS), lambda i, e, off: (e, i, 0, 0, 0)))
    idx = pl.pallas_call(
        functools.partial(_lists_body, gpt, ngroups), grid_spec=grid_spec,
        out_shape=jax.ShapeDtypeStruct(shape, jnp.int32),
        compiler_params=pltpu.CompilerParams(dimension_semantics=("arbitrary", "arbitrary")),
    )(offsets, pos)
    return idx.reshape(ne, ntiles, 1, tt)


GATHER_UNROLL = 8
SCATTER_UNROLL = 8


def _expert_body(gpt, ngroups, off_ref, h_ref, idx_ref, gate_ref, wg_ref, wu_ref, wd_ref, y_acc,
                 xe, out):
    @pl.when(pl.program_id(1) == 0)
    def _():
        y_acc[...] = jnp.zeros_like(y_acc)

    _, cnt, _ = _tile_span(off_ref, gpt, ngroups)

    def sub_tile(s, carry):
        r0 = s * EXP_ROWS

        def gather(k, c):
            for u in range(GATHER_UNROLL):
                r = k * GATHER_UNROLL + u
                xe[pl.ds(r, 1), :] = h_ref[pl.ds(idx_ref[0, r0 + r], 1), :]
            return c

        lax.fori_loop(0, EXP_ROWS // GATHER_UNROLL, gather, 0)
        w = xe[...]
        x = jnp.concatenate([pltpu.bitcast(w << 16, F32),
                             pltpu.bitcast(w & jnp.uint32(0xFFFF0000), F32)], axis=1).astype(BF16)
        he = (_silu(_dot(x, wg_ref[...])) * _dot(x, wu_ref[...])).astype(BF16)
        out[...] = _dot(he, wd_ref[...])

        def scatter(k, c):
            rows = [k * SCATTER_UNROLL + u for u in range(SCATTER_UNROLL)]
            toks = [idx_ref[0, r0 + r] for r in rows]
            gates = [jnp.where(r0 + r < cnt, gate_ref[0, t], 0.0) for r, t in zip(rows, toks)]
            vals = [y_acc[pl.ds(t, 1), :] + g * out[pl.ds(r, 1), :]
                    for r, t, g in zip(rows, toks, gates)]
            for t, v in reversed(list(zip(toks, vals))):
                y_acc[pl.ds(t, 1), :] = v
            return c

        lax.fori_loop(0, EXP_ROWS // SCATTER_UNROLL, scatter, 0)
        return carry

    lax.fori_loop(0, (cnt + EXP_ROWS - 1) // EXP_ROWS, sub_tile, 0)


def _experts(h, idx, gates, offsets, wg, wu, wd, tt):
    t = h.shape[0]
    gpt = tt // LANES
    ngroups = t // LANES
    half = D_MODEL // 2
    smem_list = pl.BlockSpec((None, None, 1, tt), lambda i, e, off: (e, i, 0, 0),
                             memory_space=pltpu.SMEM)
    grid_spec = pltpu.PrefetchScalarGridSpec(
        num_scalar_prefetch=1, grid=(t // tt, N_EXPERTS),
        in_specs=[
            pl.BlockSpec((tt, half), lambda i, e, off: (i, 0), pipeline_mode=pl.Buffered(1)),
            smem_list, smem_list,
            pl.BlockSpec((None, D_MODEL, EXPERT_FF), lambda i, e, off: (e, 0, 0)),
            pl.BlockSpec((None, D_MODEL, EXPERT_FF), lambda i, e, off: (e, 0, 0)),
            pl.BlockSpec((None, EXPERT_FF, D_MODEL), lambda i, e, off: (e, 0, 0)),
        ],
        out_specs=pl.BlockSpec((tt, D_MODEL), lambda i, e, off: (i, 0),
                               pipeline_mode=pl.Buffered(1)),
        scratch_shapes=[pltpu.VMEM((EXP_ROWS, half), jnp.uint32),
                        pltpu.VMEM((EXP_ROWS, D_MODEL), F32)],
    )
    vm = (tt * D_MODEL * (2 + 4) + 2 * 3 * D_MODEL * EXPERT_FF * 2
          + EXP_ROWS * (3 * EXPERT_FF + 4 * D_MODEL) * 4 + (4 << 20))
    return pl.pallas_call(
        functools.partial(_expert_body, gpt, ngroups), grid_spec=grid_spec,
        out_shape=jax.ShapeDtypeStruct((t, D_MODEL), F32),
        compiler_params=pltpu.CompilerParams(dimension_semantics=("arbitrary", "arbitrary"),
                                             vmem_limit_bytes=_vmem(vm)),
    )(offsets, h, idx, gates, wg, wu, wd)


def _expert_choice(h, aff_t, wg, wu, wd):
    t = h.shape[0]
    tt = min(EXP_TOKENS, t)
    cap = (EC_CAPACITY_FACTOR * t) // N_EXPERTS
    aff3 = aff_t.reshape(N_EXPERTS, t // LANES, LANES)
    pos, off = _route(aff3, cap)
    offsets = jnp.concatenate([off[:, :, 0], jnp.full((N_EXPERTS, 1), cap, jnp.int32)],
                              axis=1).reshape(-1)
    idx = _lists(pos, offsets, tt)
    gates = aff_t.reshape(N_EXPERTS, t // tt, 1, tt)
    return _experts(h, idx, gates, offsets, wg, wu, wd, tt)


def _final_body(x_ref, d_ref, nw_ref, o_ref):
    x = x_ref[...] + d_ref[...]
    ms = jnp.mean(x * x, axis=-1, keepdims=True)
    o_ref[...] = x * lax.rsqrt(ms + EPS) * nw_ref[...]


def _final(x, delta, norm_w):
    t = x.shape[0]
    tm = min(PROJ_ROWS, t)
    spec = pl.BlockSpec((tm, D_MODEL), lambda i: (i, 0))
    return pl.pallas_call(
        _final_body, grid=(t // tm,),
        in_specs=[spec, spec, pl.BlockSpec((1, D_MODEL), lambda i: (0, 0))], out_specs=spec,
        out_shape=jax.ShapeDtypeStruct((t, D_MODEL), F32),
        compiler_params=pltpu.CompilerParams(dimension_semantics=("arbitrary",)),
    )(x, delta, norm_w)


def _block_diag(w):
    nb, k, j = w.shape
    out = jnp.zeros((nb * k, nb * j), w.dtype)
    for n in range(nb):
        out = out.at[n * k:(n + 1) * k, n * j:(n + 1) * j].set(w[n])
    return out


def _rope_tables(s):
    half = HEAD_DIM // 2
    inv_freq = ROPE_BASE ** (-jnp.arange(half, dtype=F32) / half)
    ang = jnp.arange(s, dtype=F32)[:, None] * inv_freq[None, :]
    cos, sin = jnp.cos(ang), jnp.sin(ang)
    cos_t = jnp.tile(jnp.concatenate([cos, cos], axis=1), (1, N_HEADS))
    sin_t = jnp.tile(jnp.concatenate([-sin, sin], axis=1), (1, N_HEADS))
    return cos_t, sin_t


def _prepare(l, seqs, norm_mix, w_in, rg_conv_w, rg_conv_b, rg_wa, rg_ba, rg_wx, rg_bx, rg_lambda,
             hg_lb_logits, hg_norm_w, ssd_conv_w, ssd_conv_b, ssd_dt_bias, ssd_a_log, ssd_d,
             ssd_norm_w, w_out, norm_ffn, router_w, exp_w_gate, exp_w_up, exp_w_down):
    w = w_in[l]
    a_end = COLS_A
    b_end = a_end + COLS_B
    z0 = b_end
    x0 = z0 + GROUP_W
    dt0 = x0 + SSD_XBC
    d0 = dt0 + 2 * N_HEADS
    w_pad = jnp.concatenate(
        [w[:, :b_end], w[:, x0:dt0], w[:, z0:x0], w[:, dt0:d0],
         jnp.zeros((D_MODEL, LANES - 2 * N_HEADS), w.dtype), w[:, d0:]], axis=1).astype(BF16)
    row = lambda v: v.reshape(1, -1).astype(F32)
    rep = lambda v: jnp.repeat(v, HEAD_DIM).reshape(1, GROUP_W).astype(F32)
    expand = []
    for d in range(2):
        ex = np.zeros((LANES, GROUP_W), np.float32)
        for h in range(N_HEADS):
            ex[d * N_HEADS + h, h * HEAD_DIM:(h + 1) * HEAD_DIM] = 1.0
        expand.append(jnp.asarray(ex))
    bd_ones = np.kron(np.eye(N_HEADS, dtype=np.float32), np.ones((HEAD_DIM, HEAD_DIM), np.float32))
    ri, ci = np.indices((MIX_BLOCK, MIX_BLOCK))
    same_chunk = (ri // HG_CHUNK) == (ci // HG_CHUNK)
    tri_block = [jnp.asarray(m.astype(np.float32), BF16) for m in (ri >= ci, ri <= ci)]
    tri_chunk = [jnp.asarray((m & same_chunk).astype(np.float32), BF16)
                 for m in (ri >= ci, ri <= ci)]
    return {
        "norm_mix": row(norm_mix[l]), "w_pad": w_pad,
        "rg_conv_w": rg_conv_w[l], "rg_conv_b": row(rg_conv_b[l]),
        "rg_wg": [jnp.concatenate([_block_diag(rg_wa[l, d]), _block_diag(rg_wx[l, d])],
                                  axis=1).astype(BF16) for d in range(2)],
        "rg_bg": [jnp.concatenate([rg_ba[l, d], rg_bx[l, d]]).reshape(1, -1) for d in range(2)],
        "rg_lam": [row(rg_lambda[l, d]) for d in range(2)],
        "hg_lbl": [hg_lb_logits[:, d, :] for d in range(2)], "hg_norm_w": row(hg_norm_w[l]),
        "bd_f32": jnp.asarray(bd_ones), "bd_bf16": jnp.asarray(bd_ones, BF16),
        "tri_block": tri_block, "tri_chunk": tri_chunk,
        "ssd_conv_w": ssd_conv_w[l], "ssd_conv_b": row(ssd_conv_b[l]),
        "ssd_expand": [ex.astype(BF16) for ex in expand],
        "ssd_dt_bias": [rep(ssd_dt_bias[l, d]) for d in range(2)],
        "ssd_a_log": [rep(ssd_a_log[l, d]) for d in range(2)],
        "ssd_d": rep(ssd_d[l]), "ssd_norm_w": row(ssd_norm_w[l]),
        "rope": {s: _rope_tables(s) for s in seqs},
        "w_out": w_out[l].astype(BF16), "norm_ffn": row(norm_ffn[l]),
        "router_wt": router_w[l].T,
        "wg": exp_w_gate[l].astype(BF16), "wu": exp_w_up[l].astype(BF16),
        "wd": exp_w_down[l].astype(BF16),
    }


def _trunk(x3, layers, norm_final):
    b, s, _ = x3.shape
    t = b * s
    x = x3.reshape(t, D_MODEL)
    delta = None
    for l, p in enumerate(layers):
        x, (ua, ub, uc, ud) = _inproj(x, delta, p["norm_mix"], p["w_pad"])
        shape3 = lambda u: u.reshape(b, s, u.shape[-1])
        mix = (_rglru(shape3(ua), p, b, s), _hgrn(shape3(ub), p, l, b, s),
               _ssd(shape3(uc), p, b, s), _ret(shape3(ud), p, b, s))
        mix = tuple(m.reshape(t, GROUP_W) for m in mix)
        x, h, aff_t = _outproj(mix, x, p["w_out"], p["norm_ffn"], p["router_wt"])
        delta = _expert_choice(h, aff_t, p["wg"], p["wu"], p["wd"])
    return _final(x, delta, norm_final.reshape(1, -1)).reshape(b, s, D_MODEL)


def kernel(x_prompt, x_sample, norm_mix, w_in, rg_conv_w, rg_conv_b, rg_wa, rg_ba, rg_wx, rg_bx, rg_lambda, hg_lb_logits, hg_norm_w, ssd_conv_w, ssd_conv_b, ssd_dt_bias, ssd_a_log, ssd_d, ssd_norm_w, w_out, norm_ffn, router_w, exp_w_gate, exp_w_up, exp_w_down, norm_final):
    seqs = {x_prompt.shape[1], x_sample.shape[1]}
    layers = [_prepare(l, seqs, norm_mix, w_in, rg_conv_w, rg_conv_b, rg_wa, rg_ba, rg_wx, rg_bx,
                       rg_lambda, hg_lb_logits, hg_norm_w, ssd_conv_w, ssd_conv_b, ssd_dt_bias,
                       ssd_a_log, ssd_d, ssd_norm_w, w_out, norm_ffn, router_w, exp_w_gate,
                       exp_w_up, exp_w_down) for l in range(DEPTH)]
    return (_trunk(x_prompt, layers, norm_final), _trunk(x_sample, layers, norm_final))
```

```python
import functools
import math

import jax
import jax.numpy as jnp
import numpy as np
from jax import lax
from jax.experimental import pallas as pl
from jax.experimental.pallas import tpu as pltpu

F32 = jnp.float32
BF16 = jnp.bfloat16
HI = lax.Precision.HIGHEST

D_MODEL = 1024
DEPTH = 2
GROUP_W = 256
N_HEADS = 4
HEAD_DIM = 64
EPS = 1e-6
RG_C = 8.0
SSD_STATE = 128
SSD_XBC = 768
N_EXPERTS = 16
EXPERT_FF = 2048
EC_CAPACITY_FACTOR = 2
RET_DECAY_EXP = (5.0, 5.5)
ROPE_BASE = 10000.0

LANES = 128
SUBLANES = 8
HALO = SUBLANES
MIX_BLOCK = 256
MIX_BATCH = 2
HG_CHUNK = 64
HG_SUB = 16
PROJ_ROWS = 512
EXP_ROWS = 128
EXP_TOKENS = 4096
VMEM_CAP = 64 * 1024 * 1024

COLS_A = 2 * GROUP_W
COLS_B = 5 * GROUP_W
COLS_C = SSD_XBC + GROUP_W + LANES
COLS_D = 4 * GROUP_W


def _sigmoid(x):
    return 1.0 / (1.0 + jnp.exp(-x))


def _silu(x):
    return x * _sigmoid(x)


def _softplus(x):
    return jnp.maximum(x, 0.0) + jnp.log(1.0 + jnp.exp(-jnp.abs(x)))


def _gelu_tanh(x):
    return 0.5 * x * (1.0 + jnp.tanh(math.sqrt(2.0 / math.pi) * (x + 0.044715 * (x * x * x))))


def _dot(a, b, precision=None):
    return jnp.dot(a, b, preferred_element_type=F32, precision=precision)


def _dot_nt(a, b, precision=None):
    return lax.dot_general(a, b, (((1,), (1,)), ((), ())), preferred_element_type=F32,
                           precision=precision)


def _dot_tn(a, b):
    return lax.dot_general(a, b, (((0,), (0,)), ((), ())), preferred_element_type=F32)


def _head_mask(h, width=GROUP_W, head_dim=HEAD_DIM):
    lane = lax.broadcasted_iota(jnp.int32, (1, width), 1)
    return (lane // head_dim == h).astype(F32)


def _tri(n, reverse):
    r = lax.broadcasted_iota(jnp.int32, (n, n), 0)
    c = lax.broadcasted_iota(jnp.int32, (n, n), 1)
    return (r <= c) if reverse else (r >= c)


def _vmem(nbytes):
    return int(min(VMEM_CAP - (2 << 20), max(nbytes, 16 << 20)))


def _inproj_body(has_delta, *refs):
    if has_delta:
        x_ref, d_ref, nw_ref, w_ref, xo_ref, ua_ref, ub_ref, uc_ref, ud_ref = refs
        x = x_ref[...] + d_ref[...]
        xo_ref[...] = x
    else:
        x_ref, nw_ref, w_ref, ua_ref, ub_ref, uc_ref, ud_ref = refs
        x = x_ref[...]
    ms = jnp.mean(x * x, axis=-1, keepdims=True)
    h = (x * lax.rsqrt(ms + EPS) * nw_ref[...]).astype(BF16)
    c0 = 0
    for ref, n in ((ua_ref, COLS_A), (ub_ref, COLS_B), (uc_ref, COLS_C), (ud_ref, COLS_D)):
        ref[...] = _dot(h, w_ref[:, c0:c0 + n])
        c0 += n


def _inproj(x, delta, norm_w, w_pad):
    t = x.shape[0]
    tm = min(PROJ_ROWS, t)
    ncols = COLS_A + COLS_B + COLS_C + COLS_D
    row = lambda i: (i, 0)
    fixed = lambda i: (0, 0)
    xspec = pl.BlockSpec((tm, D_MODEL), row)
    in_specs = [xspec] + ([xspec] if delta is not None else []) + [
        pl.BlockSpec((1, D_MODEL), fixed), pl.BlockSpec((D_MODEL, ncols), fixed)]
    u_shapes = [jax.ShapeDtypeStruct((t, n), F32) for n in (COLS_A, COLS_B, COLS_C, COLS_D)]
    u_specs = [pl.BlockSpec((tm, n), row) for n in (COLS_A, COLS_B, COLS_C, COLS_D)]
    out_shape = ([jax.ShapeDtypeStruct((t, D_MODEL), F32)] if delta is not None else []) + u_shapes
    out_specs = ([xspec] if delta is not None else []) + u_specs
    args = (x,) + ((delta,) if delta is not None else ()) + (norm_w, w_pad)
    vm = 2 * (2 * tm * D_MODEL * 4 * 2 + D_MODEL * ncols * 2 + tm * ncols * 4) + (8 << 20)
    outs = pl.pallas_call(
        functools.partial(_inproj_body, delta is not None),
        grid=(t // tm,), in_specs=in_specs, out_specs=out_specs, out_shape=out_shape,
        compiler_params=pltpu.CompilerParams(dimension_semantics=("arbitrary",),
                                             vmem_limit_bytes=_vmem(vm)),
    )(*args)
    if delta is not None:
        return outs[0], outs[1:]
    return x, outs


def _conv4(x, prev8, next8, w, bias, first, last):
    n = x.shape[0]
    row = lax.broadcasted_iota(jnp.int32, x.shape, 0)
    pz = jnp.where(first, 0.0, prev8)
    nz = jnp.where(last, 0.0, next8)
    xm1 = jnp.where(row == 0, pz[HALO - 1:HALO], pltpu.roll(x, 1, 0))
    xm2 = jnp.where(row == 0, pz[HALO - 2:HALO - 1],
                    jnp.where(row == 1, pz[HALO - 1:HALO], pltpu.roll(x, 2, 0)))
    xp1 = jnp.where(row == n - 1, nz[0:1], pltpu.roll(x, n - 1, 0))
    return w[0:1] * xm2 + w[1:2] * xm1 + w[2:3] * x + w[3:4] * xp1 + bias


def _mixer_call(body, reverse, b, s, row_inputs, halo_inputs, const_inputs, table_inputs,
                out_dtype, state, tables=(), table_init=None, extra_row_inputs=()):
    nblk = s // MIX_BLOCK
    hb = MIX_BLOCK // HALO
    nb = MIX_BATCH if b % MIX_BATCH == 0 else 1

    def blk(j):
        return (nblk - 1 - j) if reverse else j

    in_specs, args, per_batch = [], [], []
    for arr, cb, w in tuple(row_inputs) + tuple(extra_row_inputs):
        in_specs.append(pl.BlockSpec((nb, MIX_BLOCK, w), lambda bi, j, cb=cb: (bi, blk(j), cb)))
        args.append(arr)
    for arr, cb, w in halo_inputs:
        in_specs.append(pl.BlockSpec((nb, MIX_BLOCK, w), lambda bi, j, cb=cb: (bi, blk(j), cb)))
        in_specs.append(pl.BlockSpec(
            (nb, HALO, w), lambda bi, j, cb=cb: (bi, jnp.maximum(blk(j) * hb - 1, 0), cb)))
        in_specs.append(pl.BlockSpec(
            (nb, HALO, w),
            lambda bi, j, cb=cb: (bi, jnp.minimum((blk(j) + 1) * hb, s // HALO - 1), cb)))
        args += [arr, arr, arr]
    per_batch += [True] * len(args)
    for arr in table_inputs:
        in_specs.append(pl.BlockSpec((MIX_BLOCK, arr.shape[1]), lambda bi, j: (blk(j), 0)))
        args.append(arr)
    for arr in const_inputs:
        in_specs.append(pl.BlockSpec(arr.shape, lambda bi, j, nd=arr.ndim: (0,) * nd))
        args.append(arr)
    per_batch += [False] * (len(args) - len(per_batch)) + [True] + [True] * len(state)
    per_batch += [False] * len(tables)

    def step(*refs):
        scratch_refs = refs[len(refs) - len(state) - len(tables):]

        @pl.when(pl.program_id(1) == 0)
        def _():
            for r in scratch_refs[:len(state)]:
                r[...] = jnp.zeros_like(r)

        if table_init is not None:
            @pl.when((pl.program_id(0) == 0) & (pl.program_id(1) == 0))
            def _():
                table_init(reverse, *scratch_refs[len(state):])

        for n in range(nb):
            body(reverse, nblk, *[r.at[n] if pb else r for r, pb in zip(refs, per_batch)])

    return pl.pallas_call(
        step, grid=(b // nb, nblk), in_specs=in_specs,
        out_specs=pl.BlockSpec((nb, MIX_BLOCK, GROUP_W), lambda bi, j: (bi, blk(j), 0)),
        out_shape=jax.ShapeDtypeStruct((b, s, GROUP_W), out_dtype),
        scratch_shapes=([pltpu.VMEM((nb,) + shape, dtype) for shape, dtype in state]
                        + [pltpu.VMEM(shape, dtype) for shape, dtype in tables]),
        compiler_params=pltpu.CompilerParams(dimension_semantics=("arbitrary", "arbitrary"),
                                             vmem_limit_bytes=_vmem(40 << 20)),
    )(*args)


def _edge_flags(reverse, nblk):
    j = pl.program_id(1)
    jj = (nblk - 1 - j) if reverse else j
    return jj == 0, jj == nblk - 1


def _linear_scan(a, b, reverse):
    n = a.shape[0]
    row = lax.broadcasted_iota(jnp.int32, a.shape, 0)
    d = 1
    while d < n:
        shift = (n - d) if reverse else d
        m = (row < n - d) if reverse else (row >= d)
        a_s = pltpu.roll(a, shift, 0)
        b_s = pltpu.roll(b, shift, 0)
        b = jnp.where(m, a * b_s + b, b)
        a = jnp.where(m, a * a_s, a)
        d *= 2
    return a, b


def _rglru_body(reverse, nblk, *refs):
    if reverse:
        x_ref, xp_ref, xn_ref, cw_ref, cb_ref, wg_ref, bg_ref, lam_ref, out_ref, carry = refs
    else:
        (ga_ref, hb_ref, x_ref, xp_ref, xn_ref, cw_ref, cb_ref, wg_ref, bg_ref, lam_ref,
         out_ref, carry) = refs
    first, last = _edge_flags(reverse, nblk)
    u = _conv4(x_ref[...], xp_ref[...], xn_ref[...], cw_ref[...], cb_ref[...], first, last)
    g = _dot(u.astype(BF16), wg_ref[...]) + bg_ref[...]
    r = _sigmoid(g[:, :GROUP_W])
    i = _sigmoid(g[:, GROUP_W:])
    log_a = -RG_C * r * _softplus(-lam_ref[...])
    a = jnp.exp(log_a)
    inp = jnp.sqrt(1.0 - jnp.exp(2.0 * log_a)) * i * u
    pa, h0 = _linear_scan(a, inp, reverse)
    h = h0 + pa * carry[...]
    carry[...] = h[0:1] if reverse else h[MIX_BLOCK - 1:MIX_BLOCK]
    if reverse:
        out_ref[...] = h
    else:
        out_ref[...] = ((h + hb_ref[...]) * _gelu_tanh(ga_ref[...])).astype(out_ref.dtype)


def _rglru(ua, p, b, s):
    scratch = [((1, GROUP_W), F32)]
    outs = None
    for reverse in (True, False):
        d = 1 if reverse else 0
        consts = (p["rg_conv_w"], p["rg_conv_b"], p["rg_wg"][d], p["rg_bg"][d], p["rg_lam"][d])
        if reverse:
            outs = _mixer_call(_rglru_body, True, b, s, (), ((ua, 0, GROUP_W),), consts, (), F32,
                               scratch)
        else:
            outs = _mixer_call(_rglru_body, False, b, s, ((ua, 1, GROUP_W), (outs, 0, GROUP_W)),
                               ((ua, 0, GROUP_W),), consts, (), BF16, scratch)
    return outs


def _split3(x):
    x0 = x.astype(BF16)
    r1 = x - x0.astype(F32)
    x1 = r1.astype(BF16)
    return x0, x1, (r1 - x1.astype(F32)).astype(BF16)


def _cumsum_rows(tri, x):
    return functools.reduce(lambda a, b: a + b, [_dot(tri, piece) for piece in _split3(x)])


def _select_cols(x, sel):
    return functools.reduce(lambda a, b: a + b, [_dot(piece, sel) for piece in _split3(x)])


def _head_mean_sq(o, bd):
    sq = o * o
    hi = sq.astype(BF16)
    lo = (sq - hi.astype(F32)).astype(BF16)
    return (_dot(hi, bd) + _dot(lo, bd)) * (1.0 / HEAD_DIM)


def _gla_chunk(q, k, v, cum, st, bd, reverse):
    c, nsub = HG_CHUNK, HG_CHUNK // HG_SUB
    cum_last = cum[0:1] if reverse else cum[c - 1:c]
    masks = [_head_mask(h) for h in range(N_HEADS)]
    order = list(range(nsub - 1, -1, -1)) if reverse else list(range(nsub))
    entry = {}
    for n_done, i in enumerate(order):
        r0 = i * HG_SUB
        if n_done == 0:
            entry[i] = jnp.zeros((1, GROUP_W), F32)
        else:
            entry[i] = cum[r0 + HG_SUB:r0 + HG_SUB + 1] if reverse else cum[r0 - 1:r0]
    entry_rows = jnp.concatenate(
        [jnp.broadcast_to(entry[i], (HG_SUB, GROUP_W)) for i in range(nsub)], axis=0)
    qh = q * jnp.exp(cum - entry_rows)
    kh = k * jnp.exp(entry_rows - cum)
    kh_bd = jnp.concatenate([(kh * m).astype(BF16) for m in masks], axis=0)
    pairs, lhs = [], []
    for n_done, i in enumerate(order):
        qi = qh[i * HG_SUB:(i + 1) * HG_SUB]
        for j in order[:n_done + 1]:
            pairs.append((i, j))
            lhs.append((qi if j == i else qi * jnp.exp(entry[i] - entry[j])).astype(BF16))
    res = _dot_nt(jnp.concatenate(lhs, axis=0), kh_bd)
    scol = lax.broadcasted_iota(jnp.int32, (HG_SUB, GROUP_W), 1) % HEAD_DIM
    trow = lax.broadcasted_iota(jnp.int32, (HG_SUB, GROUP_W), 0)
    blocks = [None] * nsub
    for n, (i, j) in enumerate(pairs):
        keep = scol // HG_SUB == j
        if i == j:
            keep = keep & ((scol >= trow + i * HG_SUB) if reverse else (scol <= trow + i * HG_SUB))
        piece = jnp.where(keep, res[n * HG_SUB:(n + 1) * HG_SUB], 0.0)
        blocks[i] = piece if blocks[i] is None else blocks[i] + piece
    scores = jnp.concatenate(blocks, axis=0).astype(BF16)
    v_bd = jnp.concatenate([(v * m).astype(BF16) for m in masks], axis=0)
    y = _dot(scores, v_bd)
    y = y + _dot_nt((q * jnp.exp(cum)).astype(BF16), st.astype(BF16))
    kw = (k * jnp.exp(cum_last - cum)).astype(BF16)
    st = st * jnp.exp(cum_last) + bd * _dot_tn(v.astype(BF16), kw)
    return y, st


def _hgrn_body(reverse, nblk, *refs, layer):
    if reverse:
        q_ref, f_ref, i_ref, lbl_ref, tri_ref, bd_ref, out_ref, st_ref = refs
    else:
        (q_ref, f_ref, i_ref, g_ref, ob_ref, lbl_ref, tri_ref, bd_ref, nw_ref, bdb_ref, out_ref,
         st_ref) = refs
    rows = [lbl_ref[r:r + 1, :] for r in range(DEPTH)]
    mx = functools.reduce(jnp.maximum, rows)
    es = [jnp.exp(r - mx) for r in rows]
    tot = functools.reduce(lambda x, y: x + y, es)
    sm = [e / tot for e in es]
    lb = functools.reduce(lambda x, y: x + y, sm[:layer + 1]) - sm[0]

    f = lb + (1.0 - lb) * _sigmoid(f_ref[...])
    lf = jnp.log(f)
    k = 1.0 - f
    q = q_ref[...]
    v = i_ref[...]
    nchunk = MIX_BLOCK // HG_CHUNK
    cum = _cumsum_rows(tri_ref[...], lf)
    bd = bd_ref[...]
    st = st_ref[...]
    ys = [None] * nchunk
    for c in (range(nchunk - 1, -1, -1) if reverse else range(nchunk)):
        sl = slice(c * HG_CHUNK, (c + 1) * HG_CHUNK)
        ys[c], st = _gla_chunk(q[sl], k[sl], v[sl], cum[sl], st, bd, reverse)
    st_ref[...] = st
    y = jnp.concatenate(ys, axis=0)
    if reverse:
        out_ref[...] = y
    else:
        o = y + ob_ref[...]
        o = o * lax.rsqrt(_head_mean_sq(o, bdb_ref[...]) + EPS) * nw_ref[...]
        out_ref[...] = (o * _silu(g_ref[...])).astype(out_ref.dtype)


def _hgrn(ub, p, layer, b, s):
    scratch = [((GROUP_W, GROUP_W), F32)]
    ob = _mixer_call(functools.partial(_hgrn_body, layer=layer), True, b, s,
                     ((ub, 0, GROUP_W), (ub, 2, GROUP_W), (ub, 3, GROUP_W)), (),
                     (p["hg_lbl"][1], p["tri_chunk"][1], p["bd_f32"]), (), F32, scratch)
    return _mixer_call(functools.partial(_hgrn_body, layer=layer), False, b, s,
                       ((ub, 0, GROUP_W), (ub, 1, GROUP_W), (ub, 3, GROUP_W), (ub, 4, GROUP_W),
                        (ob, 0, GROUP_W)), (),
                       (p["hg_lbl"][0], p["tri_chunk"][0], p["bd_f32"], p["hg_norm_w"],
                        p["bd_bf16"]), (), BF16, scratch)


def _ssd_body(reverse, nblk, *refs):
    if reverse:
        (dt_ref, x_ref, xp_ref, xn_ref, cw_ref, cb_ref, ex_ref, dtb_ref, alog_ref, tri_ref,
         out_ref, st_ref) = refs
    else:
        (dt_ref, z_ref, yb_ref, x_ref, xp_ref, xn_ref, cw_ref, cb_ref, ex_ref, dtb_ref, alog_ref,
         tri_ref, dsk_ref, nw_ref, out_ref, st_ref) = refs
    first, last = _edge_flags(reverse, nblk)
    n = MIX_BLOCK
    xbc = _silu(_conv4(x_ref[...], xp_ref[...], xn_ref[...], cw_ref[...], cb_ref[...], first, last))
    xs = xbc[:, :GROUP_W]
    bm = xbc[:, GROUP_W:2 * GROUP_W].astype(BF16)
    cm = xbc[:, 2 * GROUP_W:].astype(BF16)
    dt = _softplus(_select_cols(dt_ref[...], ex_ref[...]) + dtb_ref[...])
    la = dt * (-jnp.exp(alog_ref[...]))
    cum = _cumsum_rows(tri_ref[...], la)
    cum_last = cum[0:1] if reverse else cum[n - 1:n]
    cum_t = cum.T
    v = xs * dt
    keep = _tri(n, reverse)
    y = jnp.zeros((n, GROUP_W), F32)
    sc = [_dot_nt(cm[:, g * SSD_STATE:(g + 1) * SSD_STATE],
                  bm[:, g * SSD_STATE:(g + 1) * SSD_STATE]) for g in range(2)]
    for h in range(N_HEADS):
        l0 = h * HEAD_DIM
        seg = cum[:, l0:l0 + 1] - cum_t[l0:l0 + 1, :]
        dec = jnp.where(keep, jnp.exp(jnp.where(keep, seg, 0.0)), 0.0)
        y = y + _dot((sc[h // 2] * dec).astype(BF16), (v * _head_mask(h)).astype(BF16))
    st = st_ref[...]
    inter = jnp.concatenate(
        [_dot(cm[:, g * SSD_STATE:(g + 1) * SSD_STATE],
              st[:, g * SSD_STATE:(g + 1) * SSD_STATE].astype(BF16)) for g in range(2)], axis=1)
    y = y + jnp.exp(cum) * inter
    vw = (v * jnp.exp(cum_last - cum)).astype(BF16)
    upd = jnp.concatenate(
        [_dot_tn(bm[:, g * SSD_STATE:(g + 1) * SSD_STATE],
                 vw[:, g * SSD_STATE:(g + 1) * SSD_STATE]) for g in range(2)], axis=1)
    st_ref[...] = st * jnp.exp(cum_last) + upd
    if reverse:
        out_ref[...] = y
    else:
        y = (y + yb_ref[...] + dsk_ref[...] * xs) * _silu(z_ref[...])
        ms = jnp.mean(y * y, axis=-1, keepdims=True)
        out_ref[...] = (y * lax.rsqrt(ms + EPS) * nw_ref[...]).astype(out_ref.dtype)


def _ssd(uc, p, b, s):
    scratch = [((SSD_STATE, GROUP_W), F32)]
    dt_cb = (SSD_XBC + GROUP_W) // LANES
    z_cb = SSD_XBC // GROUP_W
    outs = None
    for reverse in (True, False):
        d = 1 if reverse else 0
        consts = (p["ssd_conv_w"], p["ssd_conv_b"], p["ssd_expand"][d], p["ssd_dt_bias"][d],
                  p["ssd_a_log"][d], p["tri_block"][d])
        if reverse:
            outs = _mixer_call(_ssd_body, True, b, s, ((uc, dt_cb, LANES),),
                               ((uc, 0, SSD_XBC),), consts, (), F32, scratch)
        else:
            outs = _mixer_call(_ssd_body, False, b, s,
                               ((uc, dt_cb, LANES), (uc, z_cb, GROUP_W), (outs, 0, GROUP_W)),
                               ((uc, 0, SSD_XBC),), consts + (p["ssd_d"], p["ssd_norm_w"]), (),
                               BF16, scratch)
    return outs


def _ret_log_decay(reverse):
    exp0 = RET_DECAY_EXP[1] if reverse else RET_DECAY_EXP[0]
    return [math.log1p(-2.0 ** (-exp0 - h)) for h in range(N_HEADS)]


def _ret_decay_table(reverse, dec_ref):
    n = MIX_BLOCK
    r = lax.broadcasted_iota(jnp.int32, (n, n), 0)
    c = lax.broadcasted_iota(jnp.int32, (n, n), 1)
    dist = (c - r) if reverse else (r - c)
    keep = dist >= 0
    distf = jnp.where(keep, dist, 0).astype(F32)
    for h, lg in enumerate(_ret_log_decay(reverse)):
        dec_ref[h] = jnp.where(keep, jnp.exp(distf * lg), 0.0)


def _ret_body(reverse, nblk, *refs):
    if reverse:
        q_ref, k_ref, v_ref, cos_ref, sin_ref, bd_ref, out_ref, st_ref, dec_ref = refs
    else:
        (q_ref, k_ref, v_ref, g_ref, ob_ref, cos_ref, sin_ref, bd_ref, bdb_ref, out_ref, st_ref,
         dec_ref) = refs
    n = MIX_BLOCK
    log_g = _ret_log_decay(reverse)
    lane = lax.broadcasted_iota(jnp.int32, (n, GROUP_W), 1)
    low_half = (lane % HEAD_DIM) < HEAD_DIM // 2
    cos = cos_ref[...]
    sin = sin_ref[...]

    def rot(x):
        swapped = jnp.where(low_half, pltpu.roll(x, GROUP_W - HEAD_DIM // 2, 1),
                            pltpu.roll(x, HEAD_DIM // 2, 1))
        return x * cos + swapped * sin

    lane_row = lax.broadcasted_iota(jnp.int32, (1, GROUP_W), 1) // HEAD_DIM
    la = functools.reduce(lambda acc, h: jnp.where(lane_row == h, log_g[h], acc),
                          range(N_HEADS), jnp.zeros((1, GROUP_W), F32))
    rowi = lax.broadcasted_iota(jnp.int32, (n, 1), 0)
    steps = ((n - rowi) if reverse else (rowi + 1)).astype(F32)
    cum = steps * la
    cum_last = float(n) * la
    qr = rot(q_ref[...])
    kr = rot(k_ref[...]) * (HEAD_DIM ** -0.5)
    v = v_ref[...]
    kb = kr.astype(BF16)
    y = jnp.zeros((n, GROUP_W), F32)
    for h in range(N_HEADS):
        m = _head_mask(h)
        sc = _dot_nt((qr * m).astype(BF16), kb)
        y = y + _dot((sc * dec_ref[h]).astype(BF16), (v * m).astype(BF16))
    st = st_ref[...]
    y = y + _dot_nt((qr * jnp.exp(cum)).astype(BF16), st.astype(BF16))
    kw = (kr * jnp.exp(cum_last - cum)).astype(BF16)
    st_ref[...] = st * jnp.exp(cum_last) + bd_ref[...] * _dot_tn(v.astype(BF16), kw)
    if reverse:
        out_ref[...] = y
    else:
        o = y + ob_ref[...]
        ms = _head_mean_sq(o, bdb_ref[...])
        out_ref[...] = (o * lax.rsqrt(ms + EPS) * _silu(g_ref[...])).astype(out_ref.dtype)


def _ret(ud, p, b, s):
    state = [((GROUP_W, GROUP_W), F32)]
    tables = [((N_HEADS, MIX_BLOCK, MIX_BLOCK), F32)]
    cos, sin = p["rope"][s]
    ob = _mixer_call(_ret_body, True, b, s,
                     ((ud, 0, GROUP_W), (ud, 1, GROUP_W), (ud, 2, GROUP_W)), (), (p["bd_f32"],),
                     (cos, sin), F32, state, tables, _ret_decay_table)
    return _mixer_call(_ret_body, False, b, s,
                       ((ud, 0, GROUP_W), (ud, 1, GROUP_W), (ud, 2, GROUP_W), (ud, 3, GROUP_W),
                        (ob, 0, GROUP_W)), (), (p["bd_f32"], p["bd_bf16"]), (cos, sin), BF16,
                       state, tables, _ret_decay_table)


def _outproj_body(oa_ref, ob_ref, oc_ref, od_ref, x_ref, w_ref, nw_ref, rw_ref,
                  xo_ref, h_ref, aff_ref):
    acc = x_ref[...]
    for n, ref in enumerate((oa_ref, ob_ref, oc_ref, od_ref)):
        acc = acc + _dot(ref[...], w_ref[n * GROUP_W:(n + 1) * GROUP_W, :])
    xo_ref[...] = acc
    ms = jnp.mean(acc * acc, axis=-1, keepdims=True)
    h = acc * lax.rsqrt(ms + EPS) * nw_ref[...]
    bits = pltpu.bitcast(h.astype(BF16).astype(F32), jnp.uint32)
    half = D_MODEL // 2
    h_ref[...] = (bits[:, :half] >> 16) | (bits[:, half:] & jnp.uint32(0xFFFF0000))
    logits = _dot_nt(rw_ref[...], h, HI)
    e = jnp.exp(logits - jnp.max(logits, axis=0, keepdims=True))
    aff_ref[...] = e / jnp.sum(e, axis=0, keepdims=True)


def _outproj(mix, x, w_out, norm_w, router_w):
    t = x.shape[0]
    tm = min(PROJ_ROWS, t)
    row = lambda i: (i, 0)
    fixed = lambda i: (0, 0)
    in_specs = [pl.BlockSpec((tm, GROUP_W), row)] * 4 + [
        pl.BlockSpec((tm, D_MODEL), row), pl.BlockSpec((D_MODEL, D_MODEL), fixed),
        pl.BlockSpec((1, D_MODEL), fixed), pl.BlockSpec((N_EXPERTS, D_MODEL), fixed)]
    out_specs = [pl.BlockSpec((tm, D_MODEL), row), pl.BlockSpec((tm, D_MODEL // 2), row),
                 pl.BlockSpec((N_EXPERTS, tm), lambda i: (0, i))]
    out_shape = [jax.ShapeDtypeStruct((t, D_MODEL), F32),
                 jax.ShapeDtypeStruct((t, D_MODEL // 2), jnp.uint32),
                 jax.ShapeDtypeStruct((N_EXPERTS, t), F32)]
    return pl.pallas_call(
        _outproj_body, grid=(t // tm,), in_specs=in_specs, out_specs=out_specs,
        out_shape=out_shape,
        compiler_params=pltpu.CompilerParams(dimension_semantics=("arbitrary",),
                                             vmem_limit_bytes=_vmem(40 << 20)),
    )(*mix, x, w_out, norm_w, router_w)


def _route_body(cap, aff_ref, pos_ref, off_ref):
    a = aff_ref[...]
    ng = a.shape[0]
    bits = pltpu.bitcast(a, jnp.int32)

    def count(m):
        return jnp.sum(jnp.sum(m.astype(F32), axis=0, keepdims=True), axis=1, keepdims=True)

    def step(k, prefix):
        cand = prefix | jnp.left_shift(jnp.int32(1), 30 - k)
        return jnp.where(count(bits >= cand) >= cap, cand, prefix)

    thr = lax.fori_loop(0, 31, step, jnp.zeros((1, 1), jnp.int32))
    r = lax.broadcasted_iota(jnp.int32, (LANES, LANES), 0)
    c = lax.broadcasted_iota(jnp.int32, (LANES, LANES), 1)
    incl = (r <= c).astype(BF16)
    ones = jnp.ones((LANES, LANES), BF16)
    gr = lax.broadcasted_iota(jnp.int32, (ng, ng), 0)
    gc = lax.broadcasted_iota(jnp.int32, (ng, ng), 1)
    before = (gc < gr).astype(BF16)

    def prefix(m):
        mb = m.astype(BF16)
        group_off = _dot(before, _dot(mb, ones).astype(BF16))
        return _dot(mb, incl) - m.astype(F32) + group_off, group_off

    gt = bits > thr
    eq = bits == thr
    need = cap - count(gt)
    eq_rank, _ = prefix(eq)
    sel = gt | (eq & (eq_rank < need))
    pos, group_off = prefix(sel)
    pos_ref[...] = jnp.where(sel, pos, -1.0).astype(jnp.int32)
    off_ref[...] = group_off.astype(jnp.int32)


def _route(aff3, cap):
    ne, ng, _ = aff3.shape
    spec = pl.BlockSpec((None, ng, LANES), lambda e: (e, 0, 0))
    return pl.pallas_call(
        functools.partial(_route_body, cap), grid=(ne,), in_specs=[spec], out_specs=[spec, spec],
        out_shape=[jax.ShapeDtypeStruct(aff3.shape, jnp.int32)] * 2,
        compiler_params=pltpu.CompilerParams(dimension_semantics=("arbitrary",),
                                             vmem_limit_bytes=_vmem(32 << 20)),
    )(aff3)


def _tile_span(off_ref, gpt, ngroups):
    obase = pl.program_id(1) * (ngroups + 1) + pl.program_id(0) * gpt
    base = off_ref[obase]
    return base, off_ref[obase + gpt] - base, obase


def _lists_body(gpt, ngroups, off_ref, pos_ref, idx_ref):
    base, _, obase = _tile_span(off_ref, gpt, ngroups)
    nsub = idx_ref.shape[0]
    pos_t = pos_ref[...].astype(F32).T
    tok = lax.broadcasted_iota(jnp.int32, (LANES, EXP_ROWS), 0).astype(F32)
    row = lax.broadcasted_iota(jnp.int32, (LANES, EXP_ROWS), 1).astype(F32)
    idx_ref[...] = jnp.zeros_like(idx_ref)
    for g in range(gpt):
        first = (off_ref[obase + g] - base) // EXP_ROWS
        rel = pos_t[:, g:g + 1] - base.astype(F32)
        for s in (first, first + 1):
            hit = (rel - (s * EXP_ROWS).astype(F32)) == row
            add = jnp.sum(jnp.where(hit, tok + float(g * LANES), 0.0), axis=0, keepdims=True)
            idx_ref[jnp.minimum(s, nsub - 1)] += add.astype(jnp.int32)


def _lists(pos, offsets, tt):
    ne, ngroups, _ = pos.shape
    gpt = tt // LANES
    ntiles = ngroups // gpt
    nsub = tt // EXP_ROWS
    shape = (ne, ntiles, nsub, 1, EXP_ROWS)
    grid_spec = pltpu.PrefetchScalarGridSpec(
        num_scalar_prefetch=1, grid=(ntiles, ne),
        in_specs=[pl.BlockSpec((None, gpt, LANES), lambda i, e, off: (e, i, 0))],
        out_specs=pl.BlockSpec((None, None, nsub, 1, EXP_ROWS), lambda i, e, off: (e, i, 0, 0, 0)))
    idx = pl.pallas_call(
        functools.partial(_lists_body, gpt, ngroups), grid_spec=grid_spec,
        out_shape=jax.ShapeDtypeStruct(shape, jnp.int32),
        compiler_params=pltpu.CompilerParams(dimension_semantics=("arbitrary", "arbitrary")),
    )(offsets, pos)
    return idx.reshape(ne, ntiles, 1, tt)


SCATTER_UNROLL = 8


def _expert_body(gpt, ngroups, off_ref, h_ref, idx_ref, gate_ref, wg_ref, wu_ref, wd_ref, y_acc,
                 xe, out):
    @pl.when(pl.program_id(1) == 0)
    def _():
        y_acc[...] = jnp.zeros_like(y_acc)

    @pl.when((pl.program_id(0) == 0) & (pl.program_id(1) == 0))
    def _():
        out[...] = jnp.zeros_like(out)

    _, cnt, _ = _tile_span(off_ref, gpt, ngroups)
    nsub = (cnt + EXP_ROWS - 1) // EXP_ROWS
    last_block = idx_ref.shape[1] // EXP_ROWS - 1

    def gather(dst, s):
        r0 = jnp.minimum(s, last_block) * EXP_ROWS
        for r in range(EXP_ROWS):
            dst[pl.ds(r, 1), :] = h_ref[pl.ds(idx_ref[0, r0 + r], 1), :]

    def ffn(src, dst):
        w = src[...]
        x = jnp.concatenate([pltpu.bitcast(w << 16, F32),
                             pltpu.bitcast(w & jnp.uint32(0xFFFF0000), F32)], axis=1).astype(BF16)
        he = (_silu(_dot(x, wg_ref[...])) * _dot(x, wu_ref[...])).astype(BF16)
        dst[...] = _dot(he, wd_ref[...])

    def scatter(src, s):
        live = s >= 0
        r0 = jnp.maximum(s, 0) * EXP_ROWS
        for k in range(0, EXP_ROWS, SCATTER_UNROLL):
            rows = [k + u for u in range(SCATTER_UNROLL)]
            toks = [idx_ref[0, r0 + r] for r in rows]
            gates = [jnp.where(live & (r0 + r < cnt), gate_ref[0, t], 0.0)
                     for r, t in zip(rows, toks)]
            vals = [y_acc[pl.ds(t, 1), :] + g * src[pl.ds(r, 1), :]
                    for r, t, g in zip(rows, toks, gates)]
            for t, v in reversed(list(zip(toks, vals))):
                y_acc[pl.ds(t, 1), :] = v

    @pl.when(nsub > 0)
    def _():
        gather(xe.at[0], 0)

    def sub_tile(s, carry):
        slot = s & 1
        ffn(xe.at[slot], out.at[slot])
        gather(xe.at[1 - slot], s + 1)
        scatter(out.at[1 - slot], s - 1)
        return carry

    lax.fori_loop(0, nsub, sub_tile, 0)

    @pl.when(nsub > 0)
    def _():
        scatter(out.at[(nsub - 1) & 1], nsub - 1)


def _experts(h, idx, gates, offsets, wg, wu, wd, tt):
    t = h.shape[0]
    gpt = tt // LANES
    ngroups = t // LANES
    half = D_MODEL // 2
    smem_list = pl.BlockSpec((None, None, 1, tt), lambda i, e, off: (e, i, 0, 0),
                             memory_space=pltpu.SMEM)
    grid_spec = pltpu.PrefetchScalarGridSpec(
        num_scalar_prefetch=1, grid=(t // tt, N_EXPERTS),
        in_specs=[
            pl.BlockSpec((tt, half), lambda i, e, off: (i, 0), pipeline_mode=pl.Buffered(1)),
            smem_list, smem_list,
            pl.BlockSpec((None, D_MODEL, EXPERT_FF), lambda i, e, off: (e, 0, 0)),
            pl.BlockSpec((None, D_MODEL, EXPERT_FF), lambda i, e, off: (e, 0, 0)),
            pl.BlockSpec((None, EXPERT_FF, D_MODEL), lambda i, e, off: (e, 0, 0)),
        ],
        out_specs=pl.BlockSpec((tt, D_MODEL), lambda i, e, off: (i, 0),
                               pipeline_mode=pl.Buffered(1)),
        scratch_shapes=[pltpu.VMEM((2, EXP_ROWS, half), jnp.uint32),
                        pltpu.VMEM((2, EXP_ROWS, D_MODEL), F32)],
    )
    vm = (tt * D_MODEL * (2 + 4) + 2 * 3 * D_MODEL * EXPERT_FF * 2
          + EXP_ROWS * (3 * EXPERT_FF + 4 * D_MODEL) * 4 + (4 << 20))
    return pl.pallas_call(
        functools.partial(_expert_body, gpt, ngroups), grid_spec=grid_spec,
        out_shape=jax.ShapeDtypeStruct((t, D_MODEL), F32),
        compiler_params=pltpu.CompilerParams(dimension_semantics=("arbitrary", "arbitrary"),
                                             vmem_limit_bytes=_vmem(vm)),
    )(offsets, h, idx, gates, wg, wu, wd)


def _expert_choice(h, aff_t, wg, wu, wd):
    t = h.shape[0]
    tt = min(EXP_TOKENS, t)
    cap = (EC_CAPACITY_FACTOR * t) // N_EXPERTS
    aff3 = aff_t.reshape(N_EXPERTS, t // LANES, LANES)
    pos, off = _route(aff3, cap)
    offsets = jnp.concatenate([off[:, :, 0], jnp.full((N_EXPERTS, 1), cap, jnp.int32)],
                              axis=1).reshape(-1)
    idx = _lists(pos, offsets, tt)
    gates = aff_t.reshape(N_EXPERTS, t // tt, 1, tt)
    return _experts(h, idx, gates, offsets, wg, wu, wd, tt)


def _final_body(x_ref, d_ref, nw_ref, o_ref):
    x = x_ref[...] + d_ref[...]
    ms = jnp.mean(x * x, axis=-1, keepdims=True)
    o_ref[...] = x * lax.rsqrt(ms + EPS) * nw_ref[...]


def _final(x, delta, norm_w):
    t = x.shape[0]
    tm = min(PROJ_ROWS, t)
    spec = pl.BlockSpec((tm, D_MODEL), lambda i: (i, 0))
    return pl.pallas_call(
        _final_body, grid=(t // tm,),
        in_specs=[spec, spec, pl.BlockSpec((1, D_MODEL), lambda i: (0, 0))], out_specs=spec,
        out_shape=jax.ShapeDtypeStruct((t, D_MODEL), F32),
        compiler_params=pltpu.CompilerParams(dimension_semantics=("arbitrary",)),
    )(x, delta, norm_w)


def _block_diag(w):
    nb, k, j = w.shape
    out = jnp.zeros((nb * k, nb * j), w.dtype)
    for n in range(nb):
        out = out.at[n * k:(n + 1) * k, n * j:(n + 1) * j].set(w[n])
    return out


def _rope_tables(s):
    half = HEAD_DIM // 2
    inv_freq = ROPE_BASE ** (-jnp.arange(half, dtype=F32) / half)
    ang = jnp.arange(s, dtype=F32)[:, None] * inv_freq[None, :]
    cos, sin = jnp.cos(ang), jnp.sin(ang)
    cos_t = jnp.tile(jnp.concatenate([cos, cos], axis=1), (1, N_HEADS))
    sin_t = jnp.tile(jnp.concatenate([-sin, sin], axis=1), (1, N_HEADS))
    return cos_t, sin_t


def _prepare(l, seqs, norm_mix, w_in, rg_conv_w, rg_conv_b, rg_wa, rg_ba, rg_wx, rg_bx, rg_lambda,
             hg_lb_logits, hg_norm_w, ssd_conv_w, ssd_conv_b, ssd_dt_bias, ssd_a_log, ssd_d,
             ssd_norm_w, w_out, norm_ffn, router_w, exp_w_gate, exp_w_up, exp_w_down):
    w = w_in[l]
    a_end = COLS_A
    b_end = a_end + COLS_B
    z0 = b_end
    x0 = z0 + GROUP_W
    dt0 = x0 + SSD_XBC
    d0 = dt0 + 2 * N_HEADS
    w_pad = jnp.concatenate(
        [w[:, :b_end], w[:, x0:dt0], w[:, z0:x0], w[:, dt0:d0],
         jnp.zeros((D_MODEL, LANES - 2 * N_HEADS), w.dtype), w[:, d0:]], axis=1).astype(BF16)
    row = lambda v: v.reshape(1, -1).astype(F32)
    rep = lambda v: jnp.repeat(v, HEAD_DIM).reshape(1, GROUP_W).astype(F32)
    expand = []
    for d in range(2):
        ex = np.zeros((LANES, GROUP_W), np.float32)
        for h in range(N_HEADS):
            ex[d * N_HEADS + h, h * HEAD_DIM:(h + 1) * HEAD_DIM] = 1.0
        expand.append(jnp.asarray(ex))
    bd_ones = np.kron(np.eye(N_HEADS, dtype=np.float32), np.ones((HEAD_DIM, HEAD_DIM), np.float32))
    ri, ci = np.indices((MIX_BLOCK, MIX_BLOCK))
    same_chunk = (ri // HG_CHUNK) == (ci // HG_CHUNK)
    tri_block = [jnp.asarray(m.astype(np.float32), BF16) for m in (ri >= ci, ri <= ci)]
    tri_chunk = [jnp.asarray((m & same_chunk).astype(np.float32), BF16)
                 for m in (ri >= ci, ri <= ci)]
    return {
        "norm_mix": row(norm_mix[l]), "w_pad": w_pad,
        "rg_conv_w": rg_conv_w[l], "rg_conv_b": row(rg_conv_b[l]),
        "rg_wg": [jnp.concatenate([_block_diag(rg_wa[l, d]), _block_diag(rg_wx[l, d])],
                                  axis=1).astype(BF16) for d in range(2)],
        "rg_bg": [jnp.concatenate([rg_ba[l, d], rg_bx[l, d]]).reshape(1, -1) for d in range(2)],
        "rg_lam": [row(rg_lambda[l, d]) for d in range(2)],
        "hg_lbl": [hg_lb_logits[:, d, :] for d in range(2)], "hg_norm_w": row(hg_norm_w[l]),
        "bd_f32": jnp.asarray(bd_ones), "bd_bf16": jnp.asarray(bd_ones, BF16),
        "tri_block": tri_block, "tri_chunk": tri_chunk,
        "ssd_conv_w": ssd_conv_w[l], "ssd_conv_b": row(ssd_conv_b[l]),
        "ssd_expand": [ex.astype(BF16) for ex in expand],
        "ssd_dt_bias": [rep(ssd_dt_bias[l, d]) for d in range(2)],
        "ssd_a_log": [rep(ssd_a_log[l, d]) for d in range(2)],
        "ssd_d": rep(ssd_d[l]), "ssd_norm_w": row(ssd_norm_w[l]),
        "rope": {s: _rope_tables(s) for s in seqs},
        "w_out": w_out[l].astype(BF16), "norm_ffn": row(norm_ffn[l]),
        "router_w": router_w[l].T,
        "wg": exp_w_gate[l].astype(BF16), "wu": exp_w_up[l].astype(BF16),
        "wd": exp_w_down[l].astype(BF16),
    }


def _trunk(x3, layers, norm_final):
    b, s, _ = x3.shape
    t = b * s
    x = x3.reshape(t, D_MODEL)
    delta = None
    for l, p in enumerate(layers):
        x, (ua, ub, uc, ud) = _inproj(x, delta, p["norm_mix"], p["w_pad"])
        shape3 = lambda u: u.reshape(b, s, u.shape[-1])
        mix = (_rglru(shape3(ua), p, b, s), _hgrn(shape3(ub), p, l, b, s),
               _ssd(shape3(uc), p, b, s), _ret(shape3(ud), p, b, s))
        mix = tuple(m.reshape(t, GROUP_W) for m in mix)
        x, h, aff_t = _outproj(mix, x, p["w_out"], p["norm_ffn"], p["router_w"])
        delta = _expert_choice(h, aff_t, p["wg"], p["wu"], p["wd"])
    return _final(x, delta, norm_final.reshape(1, -1)).reshape(b, s, D_MODEL)


def kernel(x_prompt, x_sample, norm_mix, w_in, rg_conv_w, rg_conv_b, rg_wa, rg_ba, rg_wx, rg_bx, rg_lambda, hg_lb_logits, hg_norm_w, ssd_conv_w, ssd_conv_b, ssd_dt_bias, ssd_a_log, ssd_d, ssd_norm_w, w_out, norm_ffn, router_w, exp_w_gate, exp_w_up, exp_w_down, norm_final):
    seqs = {x_prompt.shape[1], x_sample.shape[1]}
    layers = [_prepare(l, seqs, norm_mix, w_in, rg_conv_w, rg_conv_b, rg_wa, rg_ba, rg_wx, rg_bx,
                       rg_lambda, hg_lb_logits, hg_norm_w, ssd_conv_w, ssd_conv_b, ssd_dt_bias,
                       ssd_a_log, ssd_d, ssd_norm_w, w_out, norm_ffn, router_w, exp_w_gate,
                       exp_w_up, exp_w_down) for l in range(DEPTH)]
    return (_trunk(x_prompt, layers, norm_final), _trunk(x_sample, layers, norm_final))
```

```python
import functools
import math

import jax
import jax.numpy as jnp
import numpy as np
from jax import lax
from jax.experimental import pallas as pl
from jax.experimental.pallas import tpu as pltpu

F32 = jnp.float32
BF16 = jnp.bfloat16

D_MODEL = 1024
DEPTH = 2
GROUP_W = 256
N_HEADS = 4
HEAD_DIM = 64
EPS = 1e-6
RG_C = 8.0
SSD_STATE = 128
SSD_XBC = 768
N_EXPERTS = 16
EXPERT_FF = 2048
EC_CAPACITY_FACTOR = 2
RET_DECAY_EXP = (5.0, 5.5)
ROPE_BASE = 10000.0

LANES = 128
SUBLANES = 8
HALO = SUBLANES
MIX_BLOCK = 256
MIX_BATCH = 2
HG_CHUNK = 64
HG_SUB = 16
PROJ_ROWS = 512
EXP_ROWS = 128
EXP_TOKENS = 4096
VMEM_CAP = 64 * 1024 * 1024

COLS_A = 2 * GROUP_W
COLS_B = 5 * GROUP_W
COLS_C = SSD_XBC + GROUP_W + LANES
COLS_D = 4 * GROUP_W


def _sigmoid(x):
    return 1.0 / (1.0 + jnp.exp(-x))


def _silu(x):
    return x * _sigmoid(x)


def _softplus(x):
    return jnp.maximum(x, 0.0) + jnp.log(1.0 + jnp.exp(-jnp.abs(x)))


def _gelu_tanh(x):
    return 0.5 * x * (1.0 + jnp.tanh(math.sqrt(2.0 / math.pi) * (x + 0.044715 * (x * x * x))))


def _dot(a, b, precision=None):
    return jnp.dot(a, b, preferred_element_type=F32, precision=precision)


def _dot_nt(a, b, precision=None):
    return lax.dot_general(a, b, (((1,), (1,)), ((), ())), preferred_element_type=F32,
                           precision=precision)


def _dot_tn(a, b):
    return lax.dot_general(a, b, (((0,), (0,)), ((), ())), preferred_element_type=F32)


def _head_mask(h, width=GROUP_W, head_dim=HEAD_DIM):
    lane = lax.broadcasted_iota(jnp.int32, (1, width), 1)
    return (lane // head_dim == h).astype(F32).astype(BF16)


def _tri(n, reverse):
    r = lax.broadcasted_iota(jnp.int32, (n, n), 0)
    c = lax.broadcasted_iota(jnp.int32, (n, n), 1)
    return (r <= c) if reverse else (r >= c)


def _vmem(nbytes):
    return int(min(VMEM_CAP - (2 << 20), max(nbytes, 16 << 20)))


def _inproj_body(has_delta, *refs):
    if has_delta:
        x_ref, d_ref, nw_ref, w_ref, xo_ref, ua_ref, ub_ref, uc_ref, ud_ref = refs
        x = x_ref[...] + d_ref[...]
        xo_ref[...] = x
    else:
        x_ref, nw_ref, w_ref, ua_ref, ub_ref, uc_ref, ud_ref = refs
        x = x_ref[...]
    ms = jnp.mean(x * x, axis=-1, keepdims=True)
    h = (x * lax.rsqrt(ms + EPS) * nw_ref[...]).astype(BF16)
    c0 = 0
    for ref, n in ((ua_ref, COLS_A), (ub_ref, COLS_B), (uc_ref, COLS_C), (ud_ref, COLS_D)):
        ref[...] = _dot(h, w_ref[:, c0:c0 + n])
        c0 += n


def _inproj(x, delta, norm_w, w_pad):
    t = x.shape[0]
    tm = min(PROJ_ROWS, t)
    ncols = COLS_A + COLS_B + COLS_C + COLS_D
    row = lambda i: (i, 0)
    fixed = lambda i: (0, 0)
    xspec = pl.BlockSpec((tm, D_MODEL), row)
    in_specs = [xspec] + ([xspec] if delta is not None else []) + [
        pl.BlockSpec((1, D_MODEL), fixed), pl.BlockSpec((D_MODEL, ncols), fixed)]
    u_shapes = [jax.ShapeDtypeStruct((t, n), F32) for n in (COLS_A, COLS_B, COLS_C, COLS_D)]
    u_specs = [pl.BlockSpec((tm, n), row) for n in (COLS_A, COLS_B, COLS_C, COLS_D)]
    out_shape = ([jax.ShapeDtypeStruct((t, D_MODEL), F32)] if delta is not None else []) + u_shapes
    out_specs = ([xspec] if delta is not None else []) + u_specs
    args = (x,) + ((delta,) if delta is not None else ()) + (norm_w, w_pad)
    vm = 2 * (2 * tm * D_MODEL * 4 * 2 + D_MODEL * ncols * 2 + tm * ncols * 4) + (8 << 20)
    outs = pl.pallas_call(
        functools.partial(_inproj_body, delta is not None),
        grid=(t // tm,), in_specs=in_specs, out_specs=out_specs, out_shape=out_shape,
        compiler_params=pltpu.CompilerParams(dimension_semantics=("arbitrary",),
                                             vmem_limit_bytes=_vmem(vm)),
    )(*args)
    if delta is not None:
        return outs[0], outs[1:]
    return x, outs


def _conv4(x, prev8, next8, w, bias, first, last):
    n = x.shape[0]
    pz = jnp.where(first, 0.0, prev8)
    nz = jnp.where(last, 0.0, next8)
    y = (w[0:1] * pltpu.roll(x, 2, 0) + w[1:2] * pltpu.roll(x, 1, 0) + w[2:3] * x
         + w[3:4] * pltpu.roll(x, n - 1, 0) + bias)
    row = lax.broadcasted_iota(jnp.int32, (HALO, x.shape[1]), 0)
    dz = pz - x[n - HALO:]
    dn = nz - x[:HALO]
    top = (jnp.where(row < 2, w[0:1] * pltpu.roll(dz, 2, 0), 0.0)
           + jnp.where(row < 1, w[1:2] * pltpu.roll(dz, 1, 0), 0.0))
    bot = jnp.where(row == HALO - 1, w[3:4] * pltpu.roll(dn, HALO - 1, 0), 0.0)
    return jnp.concatenate([y[:HALO] + top, y[HALO:n - HALO], y[n - HALO:] + bot], axis=0)


def _mixer_call(body, reverse, b, s, row_inputs, halo_inputs, const_inputs, table_inputs,
                out_dtype, state, tables=(), table_init=None, extra_row_inputs=()):
    nblk = s // MIX_BLOCK
    hb = MIX_BLOCK // HALO
    nb = MIX_BATCH if b % MIX_BATCH == 0 else 1

    def blk(j):
        return (nblk - 1 - j) if reverse else j

    in_specs, args, per_batch = [], [], []
    for arr, cb, w in tuple(row_inputs) + tuple(extra_row_inputs):
        in_specs.append(pl.BlockSpec((nb, MIX_BLOCK, w), lambda bi, j, cb=cb: (bi, blk(j), cb)))
        args.append(arr)
    for arr, cb, w in halo_inputs:
        in_specs.append(pl.BlockSpec((nb, MIX_BLOCK, w), lambda bi, j, cb=cb: (bi, blk(j), cb)))
        in_specs.append(pl.BlockSpec(
            (nb, HALO, w), lambda bi, j, cb=cb: (bi, jnp.maximum(blk(j) * hb - 1, 0), cb)))
        in_specs.append(pl.BlockSpec(
            (nb, HALO, w),
            lambda bi, j, cb=cb: (bi, jnp.minimum((blk(j) + 1) * hb, s // HALO - 1), cb)))
        args += [arr, arr, arr]
    per_batch += [True] * len(args)
    for arr in table_inputs:
        in_specs.append(pl.BlockSpec((MIX_BLOCK, arr.shape[1]), lambda bi, j: (blk(j), 0)))
        args.append(arr)
    for arr in const_inputs:
        in_specs.append(pl.BlockSpec(arr.shape, lambda bi, j, nd=arr.ndim: (0,) * nd))
        args.append(arr)
    per_batch += [False] * (len(args) - len(per_batch)) + [True] + [True] * len(state)
    per_batch += [False] * len(tables)

    def step(*refs):
        scratch_refs = refs[len(refs) - len(state) - len(tables):]

        @pl.when(pl.program_id(1) == 0)
        def _():
            for r in scratch_refs[:len(state)]:
                r[...] = jnp.zeros_like(r)

        if table_init is not None:
            @pl.when((pl.program_id(0) == 0) & (pl.program_id(1) == 0))
            def _():
                table_init(reverse, *scratch_refs[len(state):])

        for n in range(nb):
            body(reverse, nblk, *[r.at[n] if pb else r for r, pb in zip(refs, per_batch)])

    return pl.pallas_call(
        step, grid=(b // nb, nblk), in_specs=in_specs,
        out_specs=pl.BlockSpec((nb, MIX_BLOCK, GROUP_W), lambda bi, j: (bi, blk(j), 0)),
        out_shape=jax.ShapeDtypeStruct((b, s, GROUP_W), out_dtype),
        scratch_shapes=([pltpu.VMEM((nb,) + shape, dtype) for shape, dtype in state]
                        + [pltpu.VMEM(shape, dtype) for shape, dtype in tables]),
        compiler_params=pltpu.CompilerParams(dimension_semantics=("arbitrary", "arbitrary"),
                                             vmem_limit_bytes=_vmem(40 << 20)),
    )(*args)


def _edge_flags(reverse, nblk):
    j = pl.program_id(1)
    jj = (nblk - 1 - j) if reverse else j
    return jj == 0, jj == nblk - 1


def _linear_scan(a, b, carry, reverse):
    n = a.shape[0]
    row = lax.broadcasted_iota(jnp.int32, a.shape, 0) % SUBLANES
    d = 1
    while d < SUBLANES:
        shift = (n - d) if reverse else d
        m = (row < SUBLANES - d) if reverse else (row >= d)
        a_s = pltpu.roll(a, shift, 0)
        b_s = pltpu.roll(b, shift, 0)
        b = jnp.where(m, a * b_s + b, b)
        a = jnp.where(m, a * a_s, a)
        d *= 2
    ngroups = n // SUBLANES
    out = [None] * ngroups
    for g in (range(ngroups - 1, -1, -1) if reverse else range(ngroups)):
        sl = slice(g * SUBLANES, (g + 1) * SUBLANES)
        h = b[sl] + a[sl] * carry
        out[g] = h
        carry = h[0:1] if reverse else h[SUBLANES - 1:SUBLANES]
    return jnp.concatenate(out, axis=0), carry


def _rglru_body(reverse, nblk, *refs):
    if reverse:
        x_ref, xp_ref, xn_ref, cw_ref, cb_ref, wg_ref, bg_ref, lam_ref, out_ref, carry = refs
    else:
        (ga_ref, hb_ref, x_ref, xp_ref, xn_ref, cw_ref, cb_ref, wg_ref, bg_ref, lam_ref,
         out_ref, carry) = refs
    first, last = _edge_flags(reverse, nblk)
    u = _conv4(x_ref[...], xp_ref[...], xn_ref[...], cw_ref[...], cb_ref[...], first, last)
    g = _dot(u.astype(BF16), wg_ref[...]) + bg_ref[...]
    r = _sigmoid(g[:, :GROUP_W])
    i = _sigmoid(g[:, GROUP_W:])
    log_a = -RG_C * r * _softplus(-lam_ref[...])
    a = jnp.exp(log_a)
    inp = jnp.sqrt(1.0 - jnp.exp(2.0 * log_a)) * i * u
    h, carry[...] = _linear_scan(a, inp, carry[...], reverse)
    if reverse:
        out_ref[...] = h
    else:
        out_ref[...] = ((h + hb_ref[...]) * _gelu_tanh(ga_ref[...])).astype(out_ref.dtype)


def _rglru(ua, p, b, s):
    scratch = [((1, GROUP_W), F32)]
    outs = None
    for reverse in (True, False):
        d = 1 if reverse else 0
        consts = (p["rg_conv_w"], p["rg_conv_b"], p["rg_wg"][d], p["rg_bg"][d], p["rg_lam"][d])
        if reverse:
            outs = _mixer_call(_rglru_body, True, b, s, (), ((ua, 0, GROUP_W),), consts, (), F32,
                               scratch)
        else:
            outs = _mixer_call(_rglru_body, False, b, s, ((ua, 1, GROUP_W), (outs, 0, GROUP_W)),
                               ((ua, 0, GROUP_W),), consts, (), BF16, scratch)
    return outs


def _split3(x):
    x0 = x.astype(BF16)
    r1 = x - x0.astype(F32)
    x1 = r1.astype(BF16)
    return x0, x1, (r1 - x1.astype(F32)).astype(BF16)


def _cumsum_rows(tri, x):
    return functools.reduce(lambda a, b: a + b, [_dot(tri, piece) for piece in _split3(x)])


def _select_cols(x, sel):
    return functools.reduce(lambda a, b: a + b, [_dot(piece, sel) for piece in _split3(x)])


def _head_mean_sq(o, bd):
    sq = o * o
    hi = sq.astype(BF16)
    lo = (sq - hi.astype(F32)).astype(BF16)
    return (_dot(hi, bd) + _dot(lo, bd)) * (1.0 / HEAD_DIM)


def _gla_chunk(q, k, v, cum, st, bd, reverse):
    c, nsub = HG_CHUNK, HG_CHUNK // HG_SUB
    cum_last = cum[0:1] if reverse else cum[c - 1:c]
    masks = [_head_mask(h) for h in range(N_HEADS)]
    order = list(range(nsub - 1, -1, -1)) if reverse else list(range(nsub))
    entry = {}
    for n_done, i in enumerate(order):
        r0 = i * HG_SUB
        if n_done == 0:
            entry[i] = jnp.zeros((1, GROUP_W), F32)
        else:
            entry[i] = cum[r0 + HG_SUB:r0 + HG_SUB + 1] if reverse else cum[r0 - 1:r0]
    entry_rows = jnp.concatenate(
        [jnp.broadcast_to(entry[i], (HG_SUB, GROUP_W)) for i in range(nsub)], axis=0)
    qh = q * jnp.exp(cum - entry_rows)
    kh = k * jnp.exp(entry_rows - cum)
    khb = kh.astype(BF16)
    kh_bd = jnp.concatenate([khb * m for m in masks], axis=0)
    pairs, lhs = [], []
    for n_done, i in enumerate(order):
        qi = qh[i * HG_SUB:(i + 1) * HG_SUB]
        for j in order[:n_done + 1]:
            pairs.append((i, j))
            lhs.append((qi if j == i else qi * jnp.exp(entry[i] - entry[j])).astype(BF16))
    res = _dot_nt(jnp.concatenate(lhs, axis=0), kh_bd)
    scol = lax.broadcasted_iota(jnp.int32, (HG_SUB, GROUP_W), 1) % HEAD_DIM
    trow = lax.broadcasted_iota(jnp.int32, (HG_SUB, GROUP_W), 0)
    blocks = [None] * nsub
    for n, (i, j) in enumerate(pairs):
        keep = scol // HG_SUB == j
        if i == j:
            keep = keep & ((scol >= trow + i * HG_SUB) if reverse else (scol <= trow + i * HG_SUB))
        piece = jnp.where(keep, res[n * HG_SUB:(n + 1) * HG_SUB], 0.0)
        blocks[i] = piece if blocks[i] is None else blocks[i] + piece
    scores = jnp.concatenate(blocks, axis=0).astype(BF16)
    vb = v.astype(BF16)
    v_bd = jnp.concatenate([vb * m for m in masks], axis=0)
    y = _dot(scores, v_bd)
    y = y + _dot_nt((q * jnp.exp(cum)).astype(BF16), st.astype(BF16))
    kw = (k * jnp.exp(cum_last - cum)).astype(BF16)
    st = st * jnp.exp(cum_last) + bd * _dot_tn(vb, kw)
    return y, st


def _hgrn_body(reverse, nblk, *refs, layer):
    if reverse:
        q_ref, f_ref, i_ref, lbl_ref, tri_ref, bd_ref, out_ref, st_ref = refs
    else:
        (q_ref, f_ref, i_ref, g_ref, ob_ref, lbl_ref, tri_ref, bd_ref, nw_ref, bdb_ref, out_ref,
         st_ref) = refs
    rows = [lbl_ref[r:r + 1, :] for r in range(DEPTH)]
    mx = functools.reduce(jnp.maximum, rows)
    es = [jnp.exp(r - mx) for r in rows]
    tot = functools.reduce(lambda x, y: x + y, es)
    sm = [e / tot for e in es]
    lb = functools.reduce(lambda x, y: x + y, sm[:layer + 1]) - sm[0]

    f = lb + (1.0 - lb) * _sigmoid(f_ref[...])
    lf = jnp.log(f)
    k = 1.0 - f
    q = q_ref[...]
    v = i_ref[...]
    nchunk = MIX_BLOCK // HG_CHUNK
    cum = _cumsum_rows(tri_ref[...], lf)
    bd = bd_ref[...]
    st = st_ref[...]
    ys = [None] * nchunk
    for c in (range(nchunk - 1, -1, -1) if reverse else range(nchunk)):
        sl = slice(c * HG_CHUNK, (c + 1) * HG_CHUNK)
        ys[c], st = _gla_chunk(q[sl], k[sl], v[sl], cum[sl], st, bd, reverse)
    st_ref[...] = st
    y = jnp.concatenate(ys, axis=0)
    if reverse:
        out_ref[...] = y
    else:
        o = y + ob_ref[...]
        o = o * lax.rsqrt(_head_mean_sq(o, bdb_ref[...]) + EPS) * nw_ref[...]
        out_ref[...] = (o * _silu(g_ref[...])).astype(out_ref.dtype)


def _hgrn(ub, p, layer, b, s):
    scratch = [((GROUP_W, GROUP_W), F32)]
    ob = _mixer_call(functools.partial(_hgrn_body, layer=layer), True, b, s,
                     ((ub, 0, GROUP_W), (ub, 2, GROUP_W), (ub, 3, GROUP_W)), (),
                     (p["hg_lbl"][1], p["tri_chunk"][1], p["bd_f32"]), (), F32, scratch)
    return _mixer_call(functools.partial(_hgrn_body, layer=layer), False, b, s,
                       ((ub, 0, GROUP_W), (ub, 1, GROUP_W), (ub, 3, GROUP_W), (ub, 4, GROUP_W),
                        (ob, 0, GROUP_W)), (),
                       (p["hg_lbl"][0], p["tri_chunk"][0], p["bd_f32"], p["hg_norm_w"],
                        p["bd_bf16"]), (), BF16, scratch)


def _ssd_body(reverse, nblk, *refs):
    if reverse:
        (dt_ref, x_ref, xp_ref, xn_ref, cw_ref, cb_ref, ex_ref, dtb_ref, alog_ref, tri_ref,
         out_ref, st_ref) = refs
    else:
        (dt_ref, z_ref, yb_ref, x_ref, xp_ref, xn_ref, cw_ref, cb_ref, ex_ref, dtb_ref, alog_ref,
         tri_ref, dsk_ref, nw_ref, out_ref, st_ref) = refs
    first, last = _edge_flags(reverse, nblk)
    n = MIX_BLOCK
    xbc = _silu(_conv4(x_ref[...], xp_ref[...], xn_ref[...], cw_ref[...], cb_ref[...], first, last))
    xs = xbc[:, :GROUP_W]
    bm = xbc[:, GROUP_W:2 * GROUP_W].astype(BF16)
    cm = xbc[:, 2 * GROUP_W:].astype(BF16)
    dt = _softplus(_select_cols(dt_ref[...], ex_ref[...]) + dtb_ref[...])
    la = dt * (-jnp.exp(alog_ref[...]))
    cum = _cumsum_rows(tri_ref[...], la)
    cum_last = cum[0:1] if reverse else cum[n - 1:n]
    cum_t = cum.T
    v = xs * dt
    vb = v.astype(BF16)
    keep = _tri(n, reverse)
    y = jnp.zeros((n, GROUP_W), F32)
    sc = [_dot_nt(cm[:, g * SSD_STATE:(g + 1) * SSD_STATE],
                  bm[:, g * SSD_STATE:(g + 1) * SSD_STATE]) for g in range(2)]
    for h in range(N_HEADS):
        l0 = h * HEAD_DIM
        seg = cum[:, l0:l0 + 1] - cum_t[l0:l0 + 1, :]
        dec = jnp.where(keep, jnp.exp(jnp.where(keep, seg, 0.0)), 0.0)
        y = y + _dot((sc[h // 2] * dec).astype(BF16), vb * _head_mask(h))
    st = st_ref[...]
    inter = jnp.concatenate(
        [_dot(cm[:, g * SSD_STATE:(g + 1) * SSD_STATE],
              st[:, g * SSD_STATE:(g + 1) * SSD_STATE].astype(BF16)) for g in range(2)], axis=1)
    y = y + jnp.exp(cum) * inter
    vw = (v * jnp.exp(cum_last - cum)).astype(BF16)
    upd = jnp.concatenate(
        [_dot_tn(bm[:, g * SSD_STATE:(g + 1) * SSD_STATE],
                 vw[:, g * SSD_STATE:(g + 1) * SSD_STATE]) for g in range(2)], axis=1)
    st_ref[...] = st * jnp.exp(cum_last) + upd
    if reverse:
        out_ref[...] = y
    else:
        y = (y + yb_ref[...] + dsk_ref[...] * xs) * _silu(z_ref[...])
        ms = jnp.mean(y * y, axis=-1, keepdims=True)
        out_ref[...] = (y * lax.rsqrt(ms + EPS) * nw_ref[...]).astype(out_ref.dtype)


def _ssd(uc, p, b, s):
    scratch = [((SSD_STATE, GROUP_W), F32)]
    dt_cb = (SSD_XBC + GROUP_W) // LANES
    z_cb = SSD_XBC // GROUP_W
    outs = None
    for reverse in (True, False):
        d = 1 if reverse else 0
        consts = (p["ssd_conv_w"], p["ssd_conv_b"], p["ssd_expand"][d], p["ssd_dt_bias"][d],
                  p["ssd_a_log"][d], p["tri_block"][d])
        if reverse:
            outs = _mixer_call(_ssd_body, True, b, s, ((uc, dt_cb, LANES),),
                               ((uc, 0, SSD_XBC),), consts, (), F32, scratch)
        else:
            outs = _mixer_call(_ssd_body, False, b, s,
                               ((uc, dt_cb, LANES), (uc, z_cb, GROUP_W), (outs, 0, GROUP_W)),
                               ((uc, 0, SSD_XBC),), consts + (p["ssd_d"], p["ssd_norm_w"]), (),
                               BF16, scratch)
    return outs


def _ret_log_decay(reverse):
    exp0 = RET_DECAY_EXP[1] if reverse else RET_DECAY_EXP[0]
    return [math.log1p(-2.0 ** (-exp0 - h)) for h in range(N_HEADS)]


def _ret_decay_table(reverse, dec_ref):
    n = MIX_BLOCK
    r = lax.broadcasted_iota(jnp.int32, (n, n), 0)
    c = lax.broadcasted_iota(jnp.int32, (n, n), 1)
    dist = (c - r) if reverse else (r - c)
    keep = dist >= 0
    distf = jnp.where(keep, dist, 0).astype(F32)
    for h, lg in enumerate(_ret_log_decay(reverse)):
        dec_ref[h] = jnp.where(keep, jnp.exp(distf * lg), 0.0)


def _ret_body(reverse, nblk, *refs):
    if reverse:
        q_ref, k_ref, v_ref, cos_ref, sin_ref, bd_ref, out_ref, st_ref, dec_ref = refs
    else:
        (q_ref, k_ref, v_ref, g_ref, ob_ref, cos_ref, sin_ref, bd_ref, bdb_ref, out_ref, st_ref,
         dec_ref) = refs
    n = MIX_BLOCK
    log_g = _ret_log_decay(reverse)
    lane = lax.broadcasted_iota(jnp.int32, (n, GROUP_W), 1)
    low_half = (lane % HEAD_DIM) < HEAD_DIM // 2
    cos = cos_ref[...]
    sin = sin_ref[...]

    def rot(x):
        swapped = jnp.where(low_half, pltpu.roll(x, GROUP_W - HEAD_DIM // 2, 1),
                            pltpu.roll(x, HEAD_DIM // 2, 1))
        return x * cos + swapped * sin

    lane_row = lax.broadcasted_iota(jnp.int32, (1, GROUP_W), 1) // HEAD_DIM
    la = functools.reduce(lambda acc, h: jnp.where(lane_row == h, log_g[h], acc),
                          range(N_HEADS), jnp.zeros((1, GROUP_W), F32))
    rowi = lax.broadcasted_iota(jnp.int32, (n, 1), 0)
    steps = ((n - rowi) if reverse else (rowi + 1)).astype(F32)
    cum = steps * la
    cum_last = float(n) * la
    qr = rot(q_ref[...])
    kr = rot(k_ref[...]) * (HEAD_DIM ** -0.5)
    v = v_ref[...]
    qb = qr.astype(BF16)
    kb = kr.astype(BF16)
    vb = v.astype(BF16)
    y = jnp.zeros((n, GROUP_W), F32)
    for h in range(N_HEADS):
        m = _head_mask(h)
        sc = _dot_nt(qb * m, kb)
        y = y + _dot((sc * dec_ref[h]).astype(BF16), vb * m)
    st = st_ref[...]
    y = y + _dot_nt((qr * jnp.exp(cum)).astype(BF16), st.astype(BF16))
    kw = (kr * jnp.exp(cum_last - cum)).astype(BF16)
    st_ref[...] = st * jnp.exp(cum_last) + bd_ref[...] * _dot_tn(vb, kw)
    if reverse:
        out_ref[...] = y
    else:
        o = y + ob_ref[...]
        ms = _head_mean_sq(o, bdb_ref[...])
        out_ref[...] = (o * lax.rsqrt(ms + EPS) * _silu(g_ref[...])).astype(out_ref.dtype)


def _ret(ud, p, b, s):
    state = [((GROUP_W, GROUP_W), F32)]
    tables = [((N_HEADS, MIX_BLOCK, MIX_BLOCK), F32)]
    cos, sin = p["rope"][s]
    ob = _mixer_call(_ret_body, True, b, s,
                     ((ud, 0, GROUP_W), (ud, 1, GROUP_W), (ud, 2, GROUP_W)), (), (p["bd_f32"],),
                     (cos, sin), F32, state, tables, _ret_decay_table)
    return _mixer_call(_ret_body, False, b, s,
                       ((ud, 0, GROUP_W), (ud, 1, GROUP_W), (ud, 2, GROUP_W), (ud, 3, GROUP_W),
                        (ob, 0, GROUP_W)), (), (p["bd_f32"], p["bd_bf16"]), (cos, sin), BF16,
                       state, tables, _ret_decay_table)


def _outproj_body(oa_ref, ob_ref, oc_ref, od_ref, x_ref, w_ref, nw_ref, rw_ref,
                  xo_ref, h_ref, aff_ref):
    acc = x_ref[...]
    for n, ref in enumerate((oa_ref, ob_ref, oc_ref, od_ref)):
        acc = acc + _dot(ref[...], w_ref[n * GROUP_W:(n + 1) * GROUP_W, :])
    xo_ref[...] = acc
    ms = jnp.mean(acc * acc, axis=-1, keepdims=True)
    h = acc * lax.rsqrt(ms + EPS) * nw_ref[...]
    bits = pltpu.bitcast(h.astype(BF16).astype(F32), jnp.uint32)
    half = D_MODEL // 2
    h_ref[...] = (bits[:, :half] >> 16) | (bits[:, half:] & jnp.uint32(0xFFFF0000))
    h0 = h.astype(BF16)
    h1 = (h - h0.astype(F32)).astype(BF16)
    rw = rw_ref[...]
    r0 = rw.astype(BF16)
    r1 = (rw - r0.astype(F32)).astype(BF16)
    logits = _dot_nt(r0, h0) + _dot_nt(r0, h1) + _dot_nt(r1, h0)
    e = jnp.exp(logits - jnp.max(logits, axis=0, keepdims=True))
    aff_ref[...] = e / jnp.sum(e, axis=0, keepdims=True)


def _outproj(mix, x, w_out, norm_w, router_w):
    t = x.shape[0]
    tm = min(PROJ_ROWS, t)
    row = lambda i: (i, 0)
    fixed = lambda i: (0, 0)
    in_specs = [pl.BlockSpec((tm, GROUP_W), row)] * 4 + [
        pl.BlockSpec((tm, D_MODEL), row), pl.BlockSpec((D_MODEL, D_MODEL), fixed),
        pl.BlockSpec((1, D_MODEL), fixed), pl.BlockSpec((N_EXPERTS, D_MODEL), fixed)]
    out_specs = [pl.BlockSpec((tm, D_MODEL), row), pl.BlockSpec((tm, D_MODEL // 2), row),
                 pl.BlockSpec((N_EXPERTS, tm), lambda i: (0, i))]
    out_shape = [jax.ShapeDtypeStruct((t, D_MODEL), F32),
                 jax.ShapeDtypeStruct((t, D_MODEL // 2), jnp.uint32),
                 jax.ShapeDtypeStruct((N_EXPERTS, t), F32)]
    return pl.pallas_call(
        _outproj_body, grid=(t // tm,), in_specs=in_specs, out_specs=out_specs,
        out_shape=out_shape,
        compiler_params=pltpu.CompilerParams(dimension_semantics=("arbitrary",),
                                             vmem_limit_bytes=_vmem(40 << 20)),
    )(*mix, x, w_out, norm_w, router_w)


def _route_body(cap, aff_ref, pos_ref, off_ref):
    a = aff_ref[...]
    ng = a.shape[0]
    bits = pltpu.bitcast(a, jnp.int32)

    def count(m):
        return jnp.sum(jnp.sum(m.astype(F32), axis=0, keepdims=True), axis=1, keepdims=True)

    def step(k, prefix):
        cand = prefix | jnp.left_shift(jnp.int32(1), 30 - k)
        return jnp.where(count(bits >= cand) >= cap, cand, prefix)

    thr = lax.fori_loop(0, 31, step, jnp.zeros((1, 1), jnp.int32))
    r = lax.broadcasted_iota(jnp.int32, (LANES, LANES), 0)
    c = lax.broadcasted_iota(jnp.int32, (LANES, LANES), 1)
    incl = (r <= c).astype(BF16)
    ones = jnp.ones((LANES, LANES), BF16)
    gr = lax.broadcasted_iota(jnp.int32, (ng, ng), 0)
    gc = lax.broadcasted_iota(jnp.int32, (ng, ng), 1)
    before = (gc < gr).astype(BF16)

    def prefix(m):
        mb = m.astype(BF16)
        group_off = _dot(before, _dot(mb, ones).astype(BF16))
        return _dot(mb, incl) - m.astype(F32) + group_off, group_off

    gt = bits > thr
    eq = bits == thr
    need = cap - count(gt)
    eq_rank, _ = prefix(eq)
    sel = gt | (eq & (eq_rank < need))
    pos, group_off = prefix(sel)
    pos_ref[...] = jnp.where(sel, pos, -1.0).astype(jnp.int32)
    off_ref[...] = group_off.astype(jnp.int32)


def _route(aff3, cap):
    ne, ng, _ = aff3.shape
    spec = pl.BlockSpec((None, ng, LANES), lambda e: (e, 0, 0))
    return pl.pallas_call(
        functools.partial(_route_body, cap), grid=(ne,), in_specs=[spec], out_specs=[spec, spec],
        out_shape=[jax.ShapeDtypeStruct(aff3.shape, jnp.int32)] * 2,
        compiler_params=pltpu.CompilerParams(dimension_semantics=("arbitrary",),
                                             vmem_limit_bytes=_vmem(32 << 20)),
    )(aff3)


def _tile_span(off_ref, gpt, ngroups):
    obase = pl.program_id(1) * (ngroups + 1) + pl.program_id(0) * gpt
    base = off_ref[obase]
    return base, off_ref[obase + gpt] - base, obase


def _lists_body(gpt, ngroups, off_ref, pos_ref, idx_ref):
    base, _, obase = _tile_span(off_ref, gpt, ngroups)
    nsub = idx_ref.shape[0]
    pos_t = pos_ref[...].astype(F32).T
    tok = lax.broadcasted_iota(jnp.int32, (LANES, EXP_ROWS), 0).astype(F32)
    row = lax.broadcasted_iota(jnp.int32, (1, EXP_ROWS), 1)
    idx_ref[...] = jnp.zeros_like(idx_ref)
    for g in range(gpt):
        first = (off_ref[obase + g] - base) // EXP_ROWS
        pos_g = jnp.broadcast_to(pos_t[:, g:g + 1], (LANES, EXP_ROWS))
        for s in (first, first + 1):
            hit = pos_g == (row + (base + s * EXP_ROWS)).astype(F32)
            add = jnp.sum(jnp.where(hit, tok + float(g * LANES), 0.0), axis=0, keepdims=True)
            idx_ref[jnp.minimum(s, nsub - 1)] += add.astype(jnp.int32)


def _lists(pos, offsets, tt):
    ne, ngroups, _ = pos.shape
    gpt = tt // LANES
    ntiles = ngroups // gpt
    nsub = tt // EXP_ROWS
    shape = (ne, ntiles, nsub, 1, EXP_ROWS)
    grid_spec = pltpu.PrefetchScalarGridSpec(
        num_scalar_prefetch=1, grid=(ntiles, ne),
        in_specs=[pl.BlockSpec((None, gpt, LANES), lambda i, e, off: (e, i, 0))],
        out_specs=pl.BlockSpec((None, None, nsub, 1, EXP_ROWS), lambda i, e, off: (e, i, 0, 0, 0)))
    idx = pl.pallas_call(
        functools.partial(_lists_body, gpt, ngroups), grid_spec=grid_spec,
        out_shape=jax.ShapeDtypeStruct(shape, jnp.int32),
        compiler_params=pltpu.CompilerParams(dimension_semantics=("arbitrary", "arbitrary")),
    )(offsets, pos)
    return idx.reshape(ne, ntiles, 1, tt)


SCATTER_UNROLL = 8


def _expert_body(gpt, ngroups, off_ref, h_ref, idx_ref, gate_ref, wg_ref, wu_ref, wd_ref, y_acc,
                 xe, out):
    @pl.when(pl.program_id(1) == 0)
    def _():
        y_acc[...] = jnp.zeros_like(y_acc)

    @pl.when((pl.program_id(0) == 0) & (pl.program_id(1) == 0))
    def _():
        out[...] = jnp.zeros_like(out)

    _, cnt, _ = _tile_span(off_ref, gpt, ngroups)
    nsub = (cnt + EXP_ROWS - 1) // EXP_ROWS
    last_block = idx_ref.shape[1] // EXP_ROWS - 1

    def gather(dst, s):
        r0 = jnp.minimum(s, last_block) * EXP_ROWS
        for r in range(EXP_ROWS):
            dst[pl.ds(r, 1), :] = h_ref[pl.ds(idx_ref[0, r0 + r], 1), :]

    def ffn(src, dst):
        w = src[...]
        x = jnp.concatenate([pltpu.bitcast(w << 16, F32),
                             pltpu.bitcast(w & jnp.uint32(0xFFFF0000), F32)], axis=1).astype(BF16)
        he = (_silu(_dot(x, wg_ref[...])) * _dot(x, wu_ref[...])).astype(BF16)
        dst[...] = _dot(he, wd_ref[...])

    def scatter(src, s):
        live = s >= 0
        r0 = jnp.maximum(s, 0) * EXP_ROWS
        for k in range(0, EXP_ROWS, SCATTER_UNROLL):
            rows = [k + u for u in range(SCATTER_UNROLL)]
            toks = [idx_ref[0, r0 + r] for r in rows]
            gates = [jnp.where(live & (r0 + r < cnt), gate_ref[0, t], 0.0)
                     for r, t in zip(rows, toks)]
            vals = [y_acc[pl.ds(t, 1), :] + g * src[pl.ds(r, 1), :]
                    for r, t, g in zip(rows, toks, gates)]
            for t, v in reversed(list(zip(toks, vals))):
                y_acc[pl.ds(t, 1), :] = v

    @pl.when(nsub > 0)
    def _():
        gather(xe.at[0], 0)

    def sub_tile(s, carry):
        slot = s & 1
        ffn(xe.at[slot], out.at[slot])
        gather(xe.at[1 - slot], s + 1)
        scatter(out.at[1 - slot], s - 1)
        return carry

    lax.fori_loop(0, nsub, sub_tile, 0)

    @pl.when(nsub > 0)
    def _():
        scatter(out.at[(nsub - 1) & 1], nsub - 1)


def _experts(h, idx, gates, offsets, wg, wu, wd, tt):
    t = h.shape[0]
    gpt = tt // LANES
    ngroups = t // LANES
    half = D_MODEL // 2
    smem_list = pl.BlockSpec((None, None, 1, tt), lambda i, e, off: (e, i, 0, 0),
                             memory_space=pltpu.SMEM)
    grid_spec = pltpu.PrefetchScalarGridSpec(
        num_scalar_prefetch=1, grid=(t // tt, N_EXPERTS),
        in_specs=[
            pl.BlockSpec((tt, half), lambda i, e, off: (i, 0), pipeline_mode=pl.Buffered(1)),
            smem_list, smem_list,
            pl.BlockSpec((None, D_MODEL, EXPERT_FF), lambda i, e, off: (e, 0, 0)),
            pl.BlockSpec((None, D_MODEL, EXPERT_FF), lambda i, e, off: (e, 0, 0)),
            pl.BlockSpec((None, EXPERT_FF, D_MODEL), lambda i, e, off: (e, 0, 0)),
        ],
        out_specs=pl.BlockSpec((tt, D_MODEL), lambda i, e, off: (i, 0),
                               pipeline_mode=pl.Buffered(1)),
        scratch_shapes=[pltpu.VMEM((2, EXP_ROWS, half), jnp.uint32),
                        pltpu.VMEM((2, EXP_ROWS, D_MODEL), F32)],
    )
    vm = (tt * D_MODEL * (2 + 4) + 2 * 3 * D_MODEL * EXPERT_FF * 2
          + EXP_ROWS * (3 * EXPERT_FF + 4 * D_MODEL) * 4 + (4 << 20))
    return pl.pallas_call(
        functools.partial(_expert_body, gpt, ngroups), grid_spec=grid_spec,
        out_shape=jax.ShapeDtypeStruct((t, D_MODEL), F32),
        compiler_params=pltpu.CompilerParams(dimension_semantics=("arbitrary", "arbitrary"),
                                             vmem_limit_bytes=_vmem(vm)),
    )(offsets, h, idx, gates, wg, wu, wd)


def _expert_choice(h, aff_t, wg, wu, wd):
    t = h.shape[0]
    tt = min(EXP_TOKENS, t)
    cap = (EC_CAPACITY_FACTOR * t) // N_EXPERTS
    aff3 = aff_t.reshape(N_EXPERTS, t // LANES, LANES)
    pos, off = _route(aff3, cap)
    offsets = jnp.concatenate([off[:, :, 0], jnp.full((N_EXPERTS, 1), cap, jnp.int32)],
                              axis=1).reshape(-1)
    idx = _lists(pos, offsets, tt)
    gates = aff_t.reshape(N_EXPERTS, t // tt, 1, tt)
    return _experts(h, idx, gates, offsets, wg, wu, wd, tt)


def _final_body(x_ref, d_ref, nw_ref, o_ref):
    x = x_ref[...] + d_ref[...]
    ms = jnp.mean(x * x, axis=-1, keepdims=True)
    o_ref[...] = x * lax.rsqrt(ms + EPS) * nw_ref[...]


def _final(x, delta, norm_w):
    t = x.shape[0]
    tm = min(PROJ_ROWS, t)
    spec = pl.BlockSpec((tm, D_MODEL), lambda i: (i, 0))
    return pl.pallas_call(
        _final_body, grid=(t // tm,),
        in_specs=[spec, spec, pl.BlockSpec((1, D_MODEL), lambda i: (0, 0))], out_specs=spec,
        out_shape=jax.ShapeDtypeStruct((t, D_MODEL), F32),
        compiler_params=pltpu.CompilerParams(dimension_semantics=("arbitrary",)),
    )(x, delta, norm_w)


def _block_diag(w):
    nb, k, j = w.shape
    out = jnp.zeros((nb * k, nb * j), w.dtype)
    for n in range(nb):
        out = out.at[n * k:(n + 1) * k, n * j:(n + 1) * j].set(w[n])
    return out


def _rope_tables(s):
    half = HEAD_DIM // 2
    inv_freq = ROPE_BASE ** (-jnp.arange(half, dtype=F32) / half)
    ang = jnp.arange(s, dtype=F32)[:, None] * inv_freq[None, :]
    cos, sin = jnp.cos(ang), jnp.sin(ang)
    cos_t = jnp.tile(jnp.concatenate([cos, cos], axis=1), (1, N_HEADS))
    sin_t = jnp.tile(jnp.concatenate([-sin, sin], axis=1), (1, N_HEADS))
    return cos_t, sin_t


def _prepare(l, seqs, norm_mix, w_in, rg_conv_w, rg_conv_b, rg_wa, rg_ba, rg_wx, rg_bx, rg_lambda,
             hg_lb_logits, hg_norm_w, ssd_conv_w, ssd_conv_b, ssd_dt_bias, ssd_a_log, ssd_d,
             ssd_norm_w, w_out, norm_ffn, router_w, exp_w_gate, exp_w_up, exp_w_down):
    w = w_in[l]
    a_end = COLS_A
    b_end = a_end + COLS_B
    z0 = b_end
    x0 = z0 + GROUP_W
    dt0 = x0 + SSD_XBC
    d0 = dt0 + 2 * N_HEADS
    w_pad = jnp.concatenate(
        [w[:, :b_end], w[:, x0:dt0], w[:, z0:x0], w[:, dt0:d0],
         jnp.zeros((D_MODEL, LANES - 2 * N_HEADS), w.dtype), w[:, d0:]], axis=1).astype(BF16)
    row = lambda v: v.reshape(1, -1).astype(F32)
    rep = lambda v: jnp.repeat(v, HEAD_DIM).reshape(1, GROUP_W).astype(F32)
    expand = []
    for d in range(2):
        ex = np.zeros((LANES, GROUP_W), np.float32)
        for h in range(N_HEADS):
            ex[d * N_HEADS + h, h * HEAD_DIM:(h + 1) * HEAD_DIM] = 1.0
        expand.append(jnp.asarray(ex))
    bd_ones = np.kron(np.eye(N_HEADS, dtype=np.float32), np.ones((HEAD_DIM, HEAD_DIM), np.float32))
    ri, ci = np.indices((MIX_BLOCK, MIX_BLOCK))
    same_chunk = (ri // HG_CHUNK) == (ci // HG_CHUNK)
    tri_block = [jnp.asarray(m.astype(np.float32), BF16) for m in (ri >= ci, ri <= ci)]
    tri_chunk = [jnp.asarray((m & same_chunk).astype(np.float32), BF16)
                 for m in (ri >= ci, ri <= ci)]
    return {
        "norm_mix": row(norm_mix[l]), "w_pad": w_pad,
        "rg_conv_w": rg_conv_w[l], "rg_conv_b": row(rg_conv_b[l]),
        "rg_wg": [jnp.concatenate([_block_diag(rg_wa[l, d]), _block_diag(rg_wx[l, d])],
                                  axis=1).astype(BF16) for d in range(2)],
        "rg_bg": [jnp.concatenate([rg_ba[l, d], rg_bx[l, d]]).reshape(1, -1) for d in range(2)],
        "rg_lam": [row(rg_lambda[l, d]) for d in range(2)],
        "hg_lbl": [hg_lb_logits[:, d, :] for d in range(2)], "hg_norm_w": row(hg_norm_w[l]),
        "bd_f32": jnp.asarray(bd_ones), "bd_bf16": jnp.asarray(bd_ones, BF16),
        "tri_block": tri_block, "tri_chunk": tri_chunk,
        "ssd_conv_w": ssd_conv_w[l], "ssd_conv_b": row(ssd_conv_b[l]),
        "ssd_expand": [ex.astype(BF16) for ex in expand],
        "ssd_dt_bias": [rep(ssd_dt_bias[l, d]) for d in range(2)],
        "ssd_a_log": [rep(ssd_a_log[l, d]) for d in range(2)],
        "ssd_d": rep(ssd_d[l]), "ssd_norm_w": row(ssd_norm_w[l]),
        "rope": {s: _rope_tables(s) for s in seqs},
        "w_out": w_out[l].astype(BF16), "norm_ffn": row(norm_ffn[l]),
        "router_w": router_w[l].T,
        "wg": exp_w_gate[l].astype(BF16), "wu": exp_w_up[l].astype(BF16),
        "wd": exp_w_down[l].astype(BF16),
    }


def _trunk(x3, layers, norm_final):
    b, s, _ = x3.shape
    t = b * s
    x = x3.reshape(t, D_MODEL)
    delta = None
    for l, p in enumerate(layers):
        x, (ua, ub, uc, ud) = _inproj(x, delta, p["norm_mix"], p["w_pad"])
        shape3 = lambda u: u.reshape(b, s, u.shape[-1])
        mix = (_rglru(shape3(ua), p, b, s), _hgrn(shape3(ub), p, l, b, s),
               _ssd(shape3(uc), p, b, s), _ret(shape3(ud), p, b, s))
        mix = tuple(m.reshape(t, GROUP_W) for m in mix)
        x, h, aff_t = _outproj(mix, x, p["w_out"], p["norm_ffn"], p["router_w"])
        delta = _expert_choice(h, aff_t, p["wg"], p["wu"], p["wd"])
    return _final(x, delta, norm_final.reshape(1, -1)).reshape(b, s, D_MODEL)


def kernel(x_prompt, x_sample, norm_mix, w_in, rg_conv_w, rg_conv_b, rg_wa, rg_ba, rg_wx, rg_bx, rg_lambda, hg_lb_logits, hg_norm_w, ssd_conv_w, ssd_conv_b, ssd_dt_bias, ssd_a_log, ssd_d, ssd_norm_w, w_out, norm_ffn, router_w, exp_w_gate, exp_w_up, exp_w_down, norm_final):
    seqs = {x_prompt.shape[1], x_sample.shape[1]}
    layers = [_prepare(l, seqs, norm_mix, w_in, rg_conv_w, rg_conv_b, rg_wa, rg_ba, rg_wx, rg_bx,
                       rg_lambda, hg_lb_logits, hg_norm_w, ssd_conv_w, ssd_conv_b, ssd_dt_bias,
                       ssd_a_log, ssd_d, ssd_norm_w, w_out, norm_ffn, router_w, exp_w_gate,
                       exp_w_up, exp_w_down) for l in range(DEPTH)]
    return (_trunk(x_prompt, layers, norm_final), _trunk(x_sample, layers, norm_final))
```

```python
import functools
import math

import jax
import jax.numpy as jnp
import numpy as np
from jax import lax
from jax.experimental import pallas as pl
from jax.experimental.pallas import tpu as pltpu

F32 = jnp.float32
BF16 = jnp.bfloat16

D_MODEL = 1024
DEPTH = 2
GROUP_W = 256
N_HEADS = 4
HEAD_DIM = 64
EPS = 1e-6
RG_C = 8.0
SSD_STATE = 128
SSD_XBC = 768
N_EXPERTS = 16
EXPERT_FF = 2048
EC_CAPACITY_FACTOR = 2
RET_DECAY_EXP = (5.0, 5.5)
ROPE_BASE = 10000.0

LANES = 128
SUBLANES = 8
HALO = SUBLANES
MIX_BLOCK = 256
MIX_BATCH = 8
HG_CHUNK = 64
HG_SUB = 16
PROJ_ROWS = 512
EXP_ROWS = 128
EXP_TOKENS = 4096
VMEM_CAP = 64 * 1024 * 1024

COLS_A = 2 * GROUP_W
COLS_B = 5 * GROUP_W
COLS_C = SSD_XBC + GROUP_W + LANES
COLS_D = 4 * GROUP_W


def _sigmoid(x):
    return 1.0 / (1.0 + jnp.exp(-x))


def _silu(x):
    return x * _sigmoid(x)


def _softplus(x):
    return jnp.maximum(x, 0.0) + jnp.log(1.0 + jnp.exp(-jnp.abs(x)))


def _gelu_tanh(x):
    return 0.5 * x * (1.0 + jnp.tanh(math.sqrt(2.0 / math.pi) * (x + 0.044715 * (x * x * x))))


def _dot(a, b, precision=None):
    return jnp.dot(a, b, preferred_element_type=F32, precision=precision)


def _dot_nt(a, b, precision=None):
    return lax.dot_general(a, b, (((1,), (1,)), ((), ())), preferred_element_type=F32,
                           precision=precision)


def _dot_tn(a, b):
    return lax.dot_general(a, b, (((0,), (0,)), ((), ())), preferred_element_type=F32)


def _head_mask(h, width=GROUP_W, head_dim=HEAD_DIM):
    lane = lax.broadcasted_iota(jnp.int32, (1, width), 1)
    return (lane // head_dim == h).astype(F32).astype(BF16)


def _tri(n, reverse):
    r = lax.broadcasted_iota(jnp.int32, (n, n), 0)
    c = lax.broadcasted_iota(jnp.int32, (n, n), 1)
    return (r <= c) if reverse else (r >= c)


def _vmem(nbytes):
    return int(min(VMEM_CAP - (2 << 20), max(nbytes, 16 << 20)))


def _inproj_body(has_delta, *refs):
    if has_delta:
        x_ref, d_ref, nw_ref, w_ref, xo_ref, ua_ref, ub_ref, uc_ref, ud_ref = refs
        x = x_ref[...] + d_ref[...]
        xo_ref[...] = x
    else:
        x_ref, nw_ref, w_ref, ua_ref, ub_ref, uc_ref, ud_ref = refs
        x = x_ref[...]
    ms = jnp.mean(x * x, axis=-1, keepdims=True)
    h = (x * lax.rsqrt(ms + EPS) * nw_ref[...]).astype(BF16)
    c0 = 0
    for ref, n in ((ua_ref, COLS_A), (ub_ref, COLS_B), (uc_ref, COLS_C), (ud_ref, COLS_D)):
        ref[...] = _dot(h, w_ref[:, c0:c0 + n])
        c0 += n


def _inproj(x, delta, norm_w, w_pad):
    t = x.shape[0]
    tm = min(PROJ_ROWS, t)
    ncols = COLS_A + COLS_B + COLS_C + COLS_D
    row = lambda i: (i, 0)
    fixed = lambda i: (0, 0)
    xspec = pl.BlockSpec((tm, D_MODEL), row)
    in_specs = [xspec] + ([xspec] if delta is not None else []) + [
        pl.BlockSpec((1, D_MODEL), fixed), pl.BlockSpec((D_MODEL, ncols), fixed)]
    u_shapes = [jax.ShapeDtypeStruct((t, n), F32) for n in (COLS_A, COLS_B, COLS_C, COLS_D)]
    u_specs = [pl.BlockSpec((tm, n), row) for n in (COLS_A, COLS_B, COLS_C, COLS_D)]
    out_shape = ([jax.ShapeDtypeStruct((t, D_MODEL), F32)] if delta is not None else []) + u_shapes
    out_specs = ([xspec] if delta is not None else []) + u_specs
    args = (x,) + ((delta,) if delta is not None else ()) + (norm_w, w_pad)
    vm = 2 * (2 * tm * D_MODEL * 4 * 2 + D_MODEL * ncols * 2 + tm * ncols * 4) + (8 << 20)
    outs = pl.pallas_call(
        functools.partial(_inproj_body, delta is not None),
        grid=(t // tm,), in_specs=in_specs, out_specs=out_specs, out_shape=out_shape,
        compiler_params=pltpu.CompilerParams(dimension_semantics=("arbitrary",),
                                             vmem_limit_bytes=_vmem(vm)),
    )(*args)
    if delta is not None:
        return outs[0], outs[1:]
    return x, outs


def _conv4(x, prev8, next8, w, bias, first, last):
    n = x.shape[0]
    pz = jnp.where(first, 0.0, prev8)
    nz = jnp.where(last, 0.0, next8)
    y = (w[0:1] * pltpu.roll(x, 2, 0) + w[1:2] * pltpu.roll(x, 1, 0) + w[2:3] * x
         + w[3:4] * pltpu.roll(x, n - 1, 0) + bias)
    row = lax.broadcasted_iota(jnp.int32, (HALO, x.shape[1]), 0)
    dz = pz - x[n - HALO:]
    dn = nz - x[:HALO]
    top = (jnp.where(row < 2, w[0:1] * pltpu.roll(dz, 2, 0), 0.0)
           + jnp.where(row < 1, w[1:2] * pltpu.roll(dz, 1, 0), 0.0))
    bot = jnp.where(row == HALO - 1, w[3:4] * pltpu.roll(dn, HALO - 1, 0), 0.0)
    return jnp.concatenate([y[:HALO] + top, y[HALO:n - HALO], y[n - HALO:] + bot], axis=0)


def _mixer_call(body, reverse, b, s, row_inputs, halo_inputs, const_inputs, table_inputs,
                out_dtype, state, tables=(), table_init=None, extra_row_inputs=()):
    nblk = s // MIX_BLOCK
    hb = MIX_BLOCK // HALO
    nb = math.gcd(b, MIX_BATCH)

    def blk(j):
        return (nblk - 1 - j) if reverse else j

    in_specs, args, per_batch = [], [], []
    for arr, cb, w in tuple(row_inputs) + tuple(extra_row_inputs):
        in_specs.append(pl.BlockSpec((nb, MIX_BLOCK, w), lambda bi, j, cb=cb: (bi, blk(j), cb)))
        args.append(arr)
    for arr, cb, w in halo_inputs:
        in_specs.append(pl.BlockSpec((nb, MIX_BLOCK, w), lambda bi, j, cb=cb: (bi, blk(j), cb)))
        in_specs.append(pl.BlockSpec(
            (nb, HALO, w), lambda bi, j, cb=cb: (bi, jnp.maximum(blk(j) * hb - 1, 0), cb)))
        in_specs.append(pl.BlockSpec(
            (nb, HALO, w),
            lambda bi, j, cb=cb: (bi, jnp.minimum((blk(j) + 1) * hb, s // HALO - 1), cb)))
        args += [arr, arr, arr]
    per_batch += [True] * len(args)
    for arr in table_inputs:
        in_specs.append(pl.BlockSpec((MIX_BLOCK, arr.shape[1]), lambda bi, j: (blk(j), 0)))
        args.append(arr)
    for arr in const_inputs:
        in_specs.append(pl.BlockSpec(arr.shape, lambda bi, j, nd=arr.ndim: (0,) * nd))
        args.append(arr)
    per_batch += [False] * (len(args) - len(per_batch)) + [True] + [True] * len(state)
    per_batch += [False] * len(tables)

    def step(*refs):
        scratch_refs = refs[len(refs) - len(state) - len(tables):]

        @pl.when(pl.program_id(1) == 0)
        def _():
            for r in scratch_refs[:len(state)]:
                r[...] = jnp.zeros_like(r)

        if table_init is not None:
            @pl.when((pl.program_id(0) == 0) & (pl.program_id(1) == 0))
            def _():
                table_init(reverse, *scratch_refs[len(state):])

        for n in range(nb):
            body(reverse, nblk, *[r.at[n] if pb else r for r, pb in zip(refs, per_batch)])

    return pl.pallas_call(
        step, grid=(b // nb, nblk), in_specs=in_specs,
        out_specs=pl.BlockSpec((nb, MIX_BLOCK, GROUP_W), lambda bi, j: (bi, blk(j), 0)),
        out_shape=jax.ShapeDtypeStruct((b, s, GROUP_W), out_dtype),
        scratch_shapes=([pltpu.VMEM((nb,) + shape, dtype) for shape, dtype in state]
                        + [pltpu.VMEM(shape, dtype) for shape, dtype in tables]),
        compiler_params=pltpu.CompilerParams(dimension_semantics=("arbitrary", "arbitrary"),
                                             vmem_limit_bytes=_vmem(40 << 20)),
    )(*args)


def _edge_flags(reverse, nblk):
    j = pl.program_id(1)
    jj = (nblk - 1 - j) if reverse else j
    return jj == 0, jj == nblk - 1


def _linear_scan(a, b, carry, reverse):
    n = a.shape[0]
    row = lax.broadcasted_iota(jnp.int32, a.shape, 0) % SUBLANES
    d = 1
    while d < SUBLANES:
        shift = (n - d) if reverse else d
        m = (row < SUBLANES - d) if reverse else (row >= d)
        a_s = pltpu.roll(a, shift, 0)
        b_s = pltpu.roll(b, shift, 0)
        b = jnp.where(m, a * b_s + b, b)
        a = jnp.where(m, a * a_s, a)
        d *= 2
    ngroups = n // SUBLANES
    out = [None] * ngroups
    for g in (range(ngroups - 1, -1, -1) if reverse else range(ngroups)):
        sl = slice(g * SUBLANES, (g + 1) * SUBLANES)
        h = b[sl] + a[sl] * carry
        out[g] = h
        carry = h[0:1] if reverse else h[SUBLANES - 1:SUBLANES]
    return jnp.concatenate(out, axis=0), carry


def _rglru_body(reverse, nblk, *refs):
    if reverse:
        x_ref, xp_ref, xn_ref, cw_ref, cb_ref, wg_ref, bg_ref, lam_ref, out_ref, carry = refs
    else:
        (ga_ref, hb_ref, x_ref, xp_ref, xn_ref, cw_ref, cb_ref, wg_ref, bg_ref, lam_ref,
         out_ref, carry) = refs
    first, last = _edge_flags(reverse, nblk)
    u = _conv4(x_ref[...], xp_ref[...], xn_ref[...], cw_ref[...], cb_ref[...], first, last)
    g = _dot(u.astype(BF16), wg_ref[...]) + bg_ref[...]
    r = _sigmoid(g[:, :GROUP_W])
    i = _sigmoid(g[:, GROUP_W:])
    log_a = -RG_C * r * _softplus(-lam_ref[...])
    a = jnp.exp(log_a)
    inp = jnp.sqrt(1.0 - jnp.exp(2.0 * log_a)) * i * u
    h, carry[...] = _linear_scan(a, inp, carry[...], reverse)
    if reverse:
        out_ref[...] = h
    else:
        out_ref[...] = ((h + hb_ref[...]) * _gelu_tanh(ga_ref[...])).astype(out_ref.dtype)


def _rglru(ua, p, b, s):
    scratch = [((1, GROUP_W), F32)]
    outs = None
    for reverse in (True, False):
        d = 1 if reverse else 0
        consts = (p["rg_conv_w"], p["rg_conv_b"], p["rg_wg"][d], p["rg_bg"][d], p["rg_lam"][d])
        if reverse:
            outs = _mixer_call(_rglru_body, True, b, s, (), ((ua, 0, GROUP_W),), consts, (), F32,
                               scratch)
        else:
            outs = _mixer_call(_rglru_body, False, b, s, ((ua, 1, GROUP_W), (outs, 0, GROUP_W)),
                               ((ua, 0, GROUP_W),), consts, (), BF16, scratch)
    return outs


def _split3(x):
    x0 = x.astype(BF16)
    r1 = x - x0.astype(F32)
    x1 = r1.astype(BF16)
    return x0, x1, (r1 - x1.astype(F32)).astype(BF16)


def _cumsum_rows(tri, x):
    return functools.reduce(lambda a, b: a + b, [_dot(tri, piece) for piece in _split3(x)])


def _select_cols(x, sel):
    return functools.reduce(lambda a, b: a + b, [_dot(piece, sel) for piece in _split3(x)])


def _head_mean_sq(o, bd):
    sq = o * o
    hi = sq.astype(BF16)
    lo = (sq - hi.astype(F32)).astype(BF16)
    return (_dot(hi, bd) + _dot(lo, bd)) * (1.0 / HEAD_DIM)


def _gla_chunk(q, k, v, cum, st, bd, reverse):
    c, nsub = HG_CHUNK, HG_CHUNK // HG_SUB
    cum_last = cum[0:1] if reverse else cum[c - 1:c]
    masks = [_head_mask(h) for h in range(N_HEADS)]
    order = list(range(nsub - 1, -1, -1)) if reverse else list(range(nsub))
    entry = {}
    for n_done, i in enumerate(order):
        r0 = i * HG_SUB
        if n_done == 0:
            entry[i] = jnp.zeros((1, GROUP_W), F32)
        else:
            entry[i] = cum[r0 + HG_SUB:r0 + HG_SUB + 1] if reverse else cum[r0 - 1:r0]
    entry_rows = jnp.concatenate(
        [jnp.broadcast_to(entry[i], (HG_SUB, GROUP_W)) for i in range(nsub)], axis=0)
    qh = q * jnp.exp(cum - entry_rows)
    kh = k * jnp.exp(entry_rows - cum)
    khb = kh.astype(BF16)
    kh_bd = jnp.concatenate([khb * m for m in masks], axis=0)
    pairs, lhs = [], []
    for n_done, i in enumerate(order):
        qi = qh[i * HG_SUB:(i + 1) * HG_SUB]
        for j in order[:n_done + 1]:
            pairs.append((i, j))
            lhs.append((qi if j == i else qi * jnp.exp(entry[i] - entry[j])).astype(BF16))
    res = _dot_nt(jnp.concatenate(lhs, axis=0), kh_bd)
    scol = lax.broadcasted_iota(jnp.int32, (HG_SUB, GROUP_W), 1) % HEAD_DIM
    trow = lax.broadcasted_iota(jnp.int32, (HG_SUB, GROUP_W), 0)
    blocks = [None] * nsub
    for n, (i, j) in enumerate(pairs):
        keep = scol // HG_SUB == j
        if i == j:
            keep = keep & ((scol >= trow + i * HG_SUB) if reverse else (scol <= trow + i * HG_SUB))
        piece = jnp.where(keep, res[n * HG_SUB:(n + 1) * HG_SUB], 0.0)
        blocks[i] = piece if blocks[i] is None else blocks[i] + piece
    scores = jnp.concatenate(blocks, axis=0).astype(BF16)
    vb = v.astype(BF16)
    v_bd = jnp.concatenate([vb * m for m in masks], axis=0)
    y = _dot(scores, v_bd)
    y = y + _dot_nt((q * jnp.exp(cum)).astype(BF16), st.astype(BF16))
    kw = (k * jnp.exp(cum_last - cum)).astype(BF16)
    st = st * jnp.exp(cum_last) + bd * _dot_tn(vb, kw)
    return y, st


def _hgrn_body(reverse, nblk, *refs, layer):
    if reverse:
        q_ref, f_ref, i_ref, lbl_ref, tri_ref, bd_ref, out_ref, st_ref = refs
    else:
        (q_ref, f_ref, i_ref, g_ref, ob_ref, lbl_ref, tri_ref, bd_ref, nw_ref, bdb_ref, out_ref,
         st_ref) = refs
    rows = [lbl_ref[r:r + 1, :] for r in range(DEPTH)]
    mx = functools.reduce(jnp.maximum, rows)
    es = [jnp.exp(r - mx) for r in rows]
    tot = functools.reduce(lambda x, y: x + y, es)
    sm = [e / tot for e in es]
    lb = functools.reduce(lambda x, y: x + y, sm[:layer + 1]) - sm[0]

    f = lb + (1.0 - lb) * _sigmoid(f_ref[...])
    lf = jnp.log(f)
    k = 1.0 - f
    q = q_ref[...]
    v = i_ref[...]
    nchunk = MIX_BLOCK // HG_CHUNK
    cum = _cumsum_rows(tri_ref[...], lf)
    bd = bd_ref[...]
    st = st_ref[...]
    ys = [None] * nchunk
    for c in (range(nchunk - 1, -1, -1) if reverse else range(nchunk)):
        sl = slice(c * HG_CHUNK, (c + 1) * HG_CHUNK)
        ys[c], st = _gla_chunk(q[sl], k[sl], v[sl], cum[sl], st, bd, reverse)
    st_ref[...] = st
    y = jnp.concatenate(ys, axis=0)
    if reverse:
        out_ref[...] = y
    else:
        o = y + ob_ref[...]
        o = o * lax.rsqrt(_head_mean_sq(o, bdb_ref[...]) + EPS) * nw_ref[...]
        out_ref[...] = (o * _silu(g_ref[...])).astype(out_ref.dtype)


def _hgrn(ub, p, layer, b, s):
    scratch = [((GROUP_W, GROUP_W), F32)]
    ob = _mixer_call(functools.partial(_hgrn_body, layer=layer), True, b, s,
                     ((ub, 0, GROUP_W), (ub, 2, GROUP_W), (ub, 3, GROUP_W)), (),
                     (p["hg_lbl"][1], p["tri_chunk"][1], p["bd_f32"]), (), F32, scratch)
    return _mixer_call(functools.partial(_hgrn_body, layer=layer), False, b, s,
                       ((ub, 0, GROUP_W), (ub, 1, GROUP_W), (ub, 3, GROUP_W), (ub, 4, GROUP_W),
                        (ob, 0, GROUP_W)), (),
                       (p["hg_lbl"][0], p["tri_chunk"][0], p["bd_f32"], p["hg_norm_w"],
                        p["bd_bf16"]), (), BF16, scratch)


def _ssd_body(reverse, nblk, *refs):
    if reverse:
        (dt_ref, x_ref, xp_ref, xn_ref, cw_ref, cb_ref, ex_ref, dtb_ref, alog_ref, tri_ref,
         out_ref, st_ref) = refs
    else:
        (dt_ref, z_ref, yb_ref, x_ref, xp_ref, xn_ref, cw_ref, cb_ref, ex_ref, dtb_ref, alog_ref,
         tri_ref, dsk_ref, nw_ref, out_ref, st_ref) = refs
    first, last = _edge_flags(reverse, nblk)
    n = MIX_BLOCK
    xbc = _silu(_conv4(x_ref[...], xp_ref[...], xn_ref[...], cw_ref[...], cb_ref[...], first, last))
    xs = xbc[:, :GROUP_W]
    bm = xbc[:, GROUP_W:2 * GROUP_W].astype(BF16)
    cm = xbc[:, 2 * GROUP_W:].astype(BF16)
    dt = _softplus(_select_cols(dt_ref[...], ex_ref[...]) + dtb_ref[...])
    la = dt * (-jnp.exp(alog_ref[...]))
    cum = _cumsum_rows(tri_ref[...], la)
    cum_last = cum[0:1] if reverse else cum[n - 1:n]
    cum_t = cum.T
    v = xs * dt
    vb = v.astype(BF16)
    keep = _tri(n, reverse)
    y = jnp.zeros((n, GROUP_W), F32)
    sc = [_dot_nt(cm[:, g * SSD_STATE:(g + 1) * SSD_STATE],
                  bm[:, g * SSD_STATE:(g + 1) * SSD_STATE]) for g in range(2)]
    for h in range(N_HEADS):
        l0 = h * HEAD_DIM
        seg = cum[:, l0:l0 + 1] - cum_t[l0:l0 + 1, :]
        dec = jnp.exp(jnp.where(keep, seg, -jnp.inf))
        y = y + _dot((sc[h // 2] * dec).astype(BF16), vb * _head_mask(h))
    st = st_ref[...]
    inter = jnp.concatenate(
        [_dot(cm[:, g * SSD_STATE:(g + 1) * SSD_STATE],
              st[:, g * SSD_STATE:(g + 1) * SSD_STATE].astype(BF16)) for g in range(2)], axis=1)
    y = y + jnp.exp(cum) * inter
    vw = (v * jnp.exp(cum_last - cum)).astype(BF16)
    upd = jnp.concatenate(
        [_dot_tn(bm[:, g * SSD_STATE:(g + 1) * SSD_STATE],
                 vw[:, g * SSD_STATE:(g + 1) * SSD_STATE]) for g in range(2)], axis=1)
    st_ref[...] = st * jnp.exp(cum_last) + upd
    if reverse:
        out_ref[...] = y
    else:
        y = (y + yb_ref[...] + dsk_ref[...] * xs) * _silu(z_ref[...])
        ms = jnp.mean(y * y, axis=-1, keepdims=True)
        out_ref[...] = (y * lax.rsqrt(ms + EPS) * nw_ref[...]).astype(out_ref.dtype)


def _ssd(uc, p, b, s):
    scratch = [((SSD_STATE, GROUP_W), F32)]
    dt_cb = (SSD_XBC + GROUP_W) // LANES
    z_cb = SSD_XBC // GROUP_W
    outs = None
    for reverse in (True, False):
        d = 1 if reverse else 0
        consts = (p["ssd_conv_w"], p["ssd_conv_b"], p["ssd_expand"][d], p["ssd_dt_bias"][d],
                  p["ssd_a_log"][d], p["tri_block"][d])
        if reverse:
            outs = _mixer_call(_ssd_body, True, b, s, ((uc, dt_cb, LANES),),
                               ((uc, 0, SSD_XBC),), consts, (), F32, scratch)
        else:
            outs = _mixer_call(_ssd_body, False, b, s,
                               ((uc, dt_cb, LANES), (uc, z_cb, GROUP_W), (outs, 0, GROUP_W)),
                               ((uc, 0, SSD_XBC),), consts + (p["ssd_d"], p["ssd_norm_w"]), (),
                               BF16, scratch)
    return outs


def _ret_log_decay(reverse):
    exp0 = RET_DECAY_EXP[1] if reverse else RET_DECAY_EXP[0]
    return [math.log1p(-2.0 ** (-exp0 - h)) for h in range(N_HEADS)]


def _ret_decay_table(reverse, dec_ref):
    n = MIX_BLOCK
    r = lax.broadcasted_iota(jnp.int32, (n, n), 0)
    c = lax.broadcasted_iota(jnp.int32, (n, n), 1)
    dist = (c - r) if reverse else (r - c)
    keep = dist >= 0
    distf = jnp.where(keep, dist, 0).astype(F32)
    for h, lg in enumerate(_ret_log_decay(reverse)):
        dec_ref[h] = jnp.where(keep, jnp.exp(distf * lg), 0.0)


def _ret_body(reverse, nblk, *refs):
    if reverse:
        q_ref, k_ref, v_ref, cos_ref, sin_ref, bd_ref, out_ref, st_ref, dec_ref = refs
    else:
        (q_ref, k_ref, v_ref, g_ref, ob_ref, cos_ref, sin_ref, bd_ref, bdb_ref, out_ref, st_ref,
         dec_ref) = refs
    n = MIX_BLOCK
    log_g = _ret_log_decay(reverse)
    lane = lax.broadcasted_iota(jnp.int32, (n, GROUP_W), 1)
    low_half = (lane % HEAD_DIM) < HEAD_DIM // 2
    cos = cos_ref[...]
    sin = sin_ref[...]

    def rot(x):
        swapped = jnp.where(low_half, pltpu.roll(x, GROUP_W - HEAD_DIM // 2, 1),
                            pltpu.roll(x, HEAD_DIM // 2, 1))
        return x * cos + swapped * sin

    lane_row = lax.broadcasted_iota(jnp.int32, (1, GROUP_W), 1) // HEAD_DIM
    la = functools.reduce(lambda acc, h: jnp.where(lane_row == h, log_g[h], acc),
                          range(N_HEADS), jnp.zeros((1, GROUP_W), F32))
    rowi = lax.broadcasted_iota(jnp.int32, (n, 1), 0)
    steps = ((n - rowi) if reverse else (rowi + 1)).astype(F32)
    cum = steps * la
    cum_last = float(n) * la
    qr = rot(q_ref[...])
    kr = rot(k_ref[...]) * (HEAD_DIM ** -0.5)
    v = v_ref[...]
    qb = qr.astype(BF16)
    kb = kr.astype(BF16)
    vb = v.astype(BF16)
    y = jnp.zeros((n, GROUP_W), F32)
    for h in range(N_HEADS):
        m = _head_mask(h)
        sc = _dot_nt(qb * m, kb)
        y = y + _dot((sc * dec_ref[h]).astype(BF16), vb * m)
    st = st_ref[...]
    y = y + _dot_nt((qr * jnp.exp(cum)).astype(BF16), st.astype(BF16))
    kw = (kr * jnp.exp(cum_last - cum)).astype(BF16)
    st_ref[...] = st * jnp.exp(cum_last) + bd_ref[...] * _dot_tn(vb, kw)
    if reverse:
        out_ref[...] = y
    else:
        o = y + ob_ref[...]
        ms = _head_mean_sq(o, bdb_ref[...])
        out_ref[...] = (o * lax.rsqrt(ms + EPS) * _silu(g_ref[...])).astype(out_ref.dtype)


def _ret(ud, p, b, s):
    state = [((GROUP_W, GROUP_W), F32)]
    tables = [((N_HEADS, MIX_BLOCK, MIX_BLOCK), F32)]
    cos, sin = p["rope"][s]
    ob = _mixer_call(_ret_body, True, b, s,
                     ((ud, 0, GROUP_W), (ud, 1, GROUP_W), (ud, 2, GROUP_W)), (), (p["bd_f32"],),
                     (cos, sin), F32, state, tables, _ret_decay_table)
    return _mixer_call(_ret_body, False, b, s,
                       ((ud, 0, GROUP_W), (ud, 1, GROUP_W), (ud, 2, GROUP_W), (ud, 3, GROUP_W),
                        (ob, 0, GROUP_W)), (), (p["bd_f32"], p["bd_bf16"]), (cos, sin), BF16,
                       state, tables, _ret_decay_table)


def _outproj_body(oa_ref, ob_ref, oc_ref, od_ref, x_ref, w_ref, nw_ref, rw_ref,
                  xo_ref, h_ref, aff_ref):
    acc = x_ref[...]
    for n, ref in enumerate((oa_ref, ob_ref, oc_ref, od_ref)):
        acc = acc + _dot(ref[...], w_ref[n * GROUP_W:(n + 1) * GROUP_W, :])
    xo_ref[...] = acc
    ms = jnp.mean(acc * acc, axis=-1, keepdims=True)
    h = acc * lax.rsqrt(ms + EPS) * nw_ref[...]
    bits = pltpu.bitcast(h.astype(BF16).astype(F32), jnp.uint32)
    half = D_MODEL // 2
    h_ref[...] = (bits[:, :half] >> 16) | (bits[:, half:] & jnp.uint32(0xFFFF0000))
    h0 = h.astype(BF16)
    h1 = (h - h0.astype(F32)).astype(BF16)
    rw = rw_ref[...]
    r0 = rw.astype(BF16)
    r1 = (rw - r0.astype(F32)).astype(BF16)
    logits = _dot_nt(r0, h0) + _dot_nt(r0, h1) + _dot_nt(r1, h0)
    e = jnp.exp(logits - jnp.max(logits, axis=0, keepdims=True))
    aff_ref[...] = e / jnp.sum(e, axis=0, keepdims=True)


def _outproj(mix, x, w_out, norm_w, router_w):
    t = x.shape[0]
    tm = min(PROJ_ROWS, t)
    row = lambda i: (i, 0)
    fixed = lambda i: (0, 0)
    in_specs = [pl.BlockSpec((tm, GROUP_W), row)] * 4 + [
        pl.BlockSpec((tm, D_MODEL), row), pl.BlockSpec((D_MODEL, D_MODEL), fixed),
        pl.BlockSpec((1, D_MODEL), fixed), pl.BlockSpec((N_EXPERTS, D_MODEL), fixed)]
    out_specs = [pl.BlockSpec((tm, D_MODEL), row), pl.BlockSpec((tm, D_MODEL // 2), row),
                 pl.BlockSpec((N_EXPERTS, tm), lambda i: (0, i))]
    out_shape = [jax.ShapeDtypeStruct((t, D_MODEL), F32),
                 jax.ShapeDtypeStruct((t, D_MODEL // 2), jnp.uint32),
                 jax.ShapeDtypeStruct((N_EXPERTS, t), F32)]
    return pl.pallas_call(
        _outproj_body, grid=(t // tm,), in_specs=in_specs, out_specs=out_specs,
        out_shape=out_shape,
        compiler_params=pltpu.CompilerParams(dimension_semantics=("arbitrary",),
                                             vmem_limit_bytes=_vmem(40 << 20)),
    )(*mix, x, w_out, norm_w, router_w)


def _route_body(cap, aff_ref, pos_ref, off_ref):
    a = aff_ref[...]
    ng = a.shape[0]
    bits = pltpu.bitcast(a, jnp.int32)

    def count(m):
        return jnp.sum(jnp.sum(m.astype(F32), axis=0, keepdims=True), axis=1, keepdims=True)

    def step(k, prefix):
        cand = prefix | jnp.left_shift(jnp.int32(1), 30 - k)
        return jnp.where(count(bits >= cand) >= cap, cand, prefix)

    thr = lax.fori_loop(0, 31, step, jnp.zeros((1, 1), jnp.int32))
    r = lax.broadcasted_iota(jnp.int32, (LANES, LANES), 0)
    c = lax.broadcasted_iota(jnp.int32, (LANES, LANES), 1)
    incl = (r <= c).astype(BF16)
    ones = jnp.ones((LANES, LANES), BF16)
    gr = lax.broadcasted_iota(jnp.int32, (ng, ng), 0)
    gc = lax.broadcasted_iota(jnp.int32, (ng, ng), 1)
    before = (gc < gr).astype(BF16)

    def prefix(m):
        mb = m.astype(BF16)
        group_off = _dot(before, _dot(mb, ones).astype(BF16))
        return _dot(mb, incl) - m.astype(F32) + group_off, group_off

    gt = bits > thr
    eq = bits == thr
    need = cap - count(gt)
    eq_rank, _ = prefix(eq)
    sel = gt | (eq & (eq_rank < need))
    pos, group_off = prefix(sel)
    pos_ref[...] = jnp.where(sel, pos, -1.0).astype(jnp.int32)
    off_ref[...] = group_off.astype(jnp.int32)


def _route(aff3, cap):
    ne, ng, _ = aff3.shape
    spec = pl.BlockSpec((None, ng, LANES), lambda e: (e, 0, 0))
    return pl.pallas_call(
        functools.partial(_route_body, cap), grid=(ne,), in_specs=[spec], out_specs=[spec, spec],
        out_shape=[jax.ShapeDtypeStruct(aff3.shape, jnp.int32)] * 2,
        compiler_params=pltpu.CompilerParams(dimension_semantics=("arbitrary",),
                                             vmem_limit_bytes=_vmem(32 << 20)),
    )(aff3)


def _tile_span(off_ref, gpt, ngroups):
    obase = pl.program_id(1) * (ngroups + 1) + pl.program_id(0) * gpt
    base = off_ref[obase]
    return base, off_ref[obase + gpt] - base, obase


def _lists_body(gpt, ngroups, off_ref, pos_ref, idx_ref):
    base, _, obase = _tile_span(off_ref, gpt, ngroups)
    nsub = idx_ref.shape[0]
    pos_t = pos_ref[...].astype(F32).T
    tok = lax.broadcasted_iota(jnp.int32, (LANES, EXP_ROWS), 0).astype(F32)
    row = lax.broadcasted_iota(jnp.int32, (1, EXP_ROWS), 1)
    idx_ref[...] = jnp.zeros_like(idx_ref)
    for g in range(gpt):
        first = (off_ref[obase + g] - base) // EXP_ROWS
        pos_g = jnp.broadcast_to(pos_t[:, g:g + 1], (LANES, EXP_ROWS))
        for s in (first, first + 1):
            hit = pos_g == (row + (base + s * EXP_ROWS)).astype(F32)
            add = jnp.sum(jnp.where(hit, tok + float(g * LANES), 0.0), axis=0, keepdims=True)
            idx_ref[jnp.minimum(s, nsub - 1)] += add.astype(jnp.int32)


def _lists(pos, offsets, tt):
    ne, ngroups, _ = pos.shape
    gpt = tt // LANES
    ntiles = ngroups // gpt
    nsub = tt // EXP_ROWS
    shape = (ne, ntiles, nsub, 1, EXP_ROWS)
    grid_spec = pltpu.PrefetchScalarGridSpec(
        num_scalar_prefetch=1, grid=(ntiles, ne),
        in_specs=[pl.BlockSpec((None, gpt, LANES), lambda i, e, off: (e, i, 0))],
        out_specs=pl.BlockSpec((None, None, nsub, 1, EXP_ROWS), lambda i, e, off: (e, i, 0, 0, 0)))
    idx = pl.pallas_call(
        functools.partial(_lists_body, gpt, ngroups), grid_spec=grid_spec,
        out_shape=jax.ShapeDtypeStruct(shape, jnp.int32),
        compiler_params=pltpu.CompilerParams(dimension_semantics=("arbitrary", "arbitrary")),
    )(offsets, pos)
    return idx.reshape(ne, ntiles, 1, tt)


SCATTER_UNROLL = 4


def _expert_body(gpt, ngroups, off_ref, h_ref, idx_ref, gate_ref, wg_ref, wu_ref, wd_ref, y_acc,
                 xe, out):
    @pl.when(pl.program_id(1) == 0)
    def _():
        y_acc[...] = jnp.zeros_like(y_acc)

    @pl.when((pl.program_id(0) == 0) & (pl.program_id(1) == 0))
    def _():
        out[...] = jnp.zeros_like(out)

    _, cnt, _ = _tile_span(off_ref, gpt, ngroups)
    nsub = (cnt + EXP_ROWS - 1) // EXP_ROWS
    last_block = idx_ref.shape[1] // EXP_ROWS - 1

    def gather(dst, s):
        r0 = jnp.minimum(s, last_block) * EXP_ROWS
        for r in range(EXP_ROWS):
            dst[pl.ds(r, 1), :] = h_ref[pl.ds(idx_ref[0, r0 + r], 1), :]

    def ffn(src, dst):
        w = src[...]
        x = jnp.concatenate([pltpu.bitcast(w << 16, F32),
                             pltpu.bitcast(w & jnp.uint32(0xFFFF0000), F32)], axis=1).astype(BF16)
        he = (_silu(_dot(x, wg_ref[...])) * _dot(x, wu_ref[...])).astype(BF16)
        dst[...] = _dot(he, wd_ref[...])

    def scatter(src, s):
        live = s >= 0
        r0 = jnp.maximum(s, 0) * EXP_ROWS
        for k in range(0, EXP_ROWS, SCATTER_UNROLL):
            rows = [k + u for u in range(SCATTER_UNROLL)]
            toks = [idx_ref[0, r0 + r] for r in rows]
            gates = [jnp.where(live & (r0 + r < cnt), gate_ref[0, t], 0.0)
                     for r, t in zip(rows, toks)]
            vals = [y_acc[pl.ds(t, 1), :] + g * src[pl.ds(r, 1), :]
                    for r, t, g in zip(rows, toks, gates)]
            for t, v in reversed(list(zip(toks, vals))):
                y_acc[pl.ds(t, 1), :] = v

    @pl.when(nsub > 0)
    def _():
        gather(xe.at[0], 0)

    def sub_tile(s, carry):
        slot = s & 1
        ffn(xe.at[slot], out.at[slot])
        gather(xe.at[1 - slot], s + 1)
        scatter(out.at[1 - slot], s - 1)
        return carry

    lax.fori_loop(0, nsub, sub_tile, 0)

    @pl.when(nsub > 0)
    def _():
        scatter(out.at[(nsub - 1) & 1], nsub - 1)


def _experts(h, idx, gates, offsets, wg, wu, wd, tt):
    t = h.shape[0]
    gpt = tt // LANES
    ngroups = t // LANES
    half = D_MODEL // 2
    smem_list = pl.BlockSpec((None, None, 1, tt), lambda i, e, off: (e, i, 0, 0),
                             memory_space=pltpu.SMEM)
    grid_spec = pltpu.PrefetchScalarGridSpec(
        num_scalar_prefetch=1, grid=(t // tt, N_EXPERTS),
        in_specs=[
            pl.BlockSpec((tt, half), lambda i, e, off: (i, 0), pipeline_mode=pl.Buffered(1)),
            smem_list, smem_list,
            pl.BlockSpec((None, D_MODEL, EXPERT_FF), lambda i, e, off: (e, 0, 0)),
            pl.BlockSpec((None, D_MODEL, EXPERT_FF), lambda i, e, off: (e, 0, 0)),
            pl.BlockSpec((None, EXPERT_FF, D_MODEL), lambda i, e, off: (e, 0, 0)),
        ],
        out_specs=pl.BlockSpec((tt, D_MODEL), lambda i, e, off: (i, 0),
                               pipeline_mode=pl.Buffered(1)),
        scratch_shapes=[pltpu.VMEM((2, EXP_ROWS, half), jnp.uint32),
                        pltpu.VMEM((2, EXP_ROWS, D_MODEL), F32)],
    )
    vm = (tt * D_MODEL * (2 + 4) + 2 * 3 * D_MODEL * EXPERT_FF * 2
          + EXP_ROWS * (3 * EXPERT_FF + 4 * D_MODEL) * 4 + (4 << 20))
    return pl.pallas_call(
        functools.partial(_expert_body, gpt, ngroups), grid_spec=grid_spec,
        out_shape=jax.ShapeDtypeStruct((t, D_MODEL), F32),
        compiler_params=pltpu.CompilerParams(dimension_semantics=("arbitrary", "arbitrary"),
                                             vmem_limit_bytes=_vmem(vm)),
    )(offsets, h, idx, gates, wg, wu, wd)


def _expert_choice(h, aff_t, wg, wu, wd):
    t = h.shape[0]
    tt = min(EXP_TOKENS, t)
    cap = (EC_CAPACITY_FACTOR * t) // N_EXPERTS
    aff3 = aff_t.reshape(N_EXPERTS, t // LANES, LANES)
    pos, off = _route(aff3, cap)
    offsets = jnp.concatenate([off[:, :, 0], jnp.full((N_EXPERTS, 1), cap, jnp.int32)],
                              axis=1).reshape(-1)
    idx = _lists(pos, offsets, tt)
    gates = aff_t.reshape(N_EXPERTS, t // tt, 1, tt)
    return _experts(h, idx, gates, offsets, wg, wu, wd, tt)


def _final_body(x_ref, d_ref, nw_ref, o_ref):
    x = x_ref[...] + d_ref[...]
    ms = jnp.mean(x * x, axis=-1, keepdims=True)
    o_ref[...] = x * lax.rsqrt(ms + EPS) * nw_ref[...]


def _final(x, delta, norm_w):
    t = x.shape[0]
    tm = min(PROJ_ROWS, t)
    spec = pl.BlockSpec((tm, D_MODEL), lambda i: (i, 0))
    return pl.pallas_call(
        _final_body, grid=(t // tm,),
        in_specs=[spec, spec, pl.BlockSpec((1, D_MODEL), lambda i: (0, 0))], out_specs=spec,
        out_shape=jax.ShapeDtypeStruct((t, D_MODEL), F32),
        compiler_params=pltpu.CompilerParams(dimension_semantics=("arbitrary",)),
    )(x, delta, norm_w)


def _block_diag(w):
    nb, k, j = w.shape
    out = jnp.zeros((nb * k, nb * j), w.dtype)
    for n in range(nb):
        out = out.at[n * k:(n + 1) * k, n * j:(n + 1) * j].set(w[n])
    return out


def _rope_tables(s):
    half = HEAD_DIM // 2
    inv_freq = ROPE_BASE ** (-jnp.arange(half, dtype=F32) / half)
    ang = jnp.arange(s, dtype=F32)[:, None] * inv_freq[None, :]
    cos, sin = jnp.cos(ang), jnp.sin(ang)
    cos_t = jnp.tile(jnp.concatenate([cos, cos], axis=1), (1, N_HEADS))
    sin_t = jnp.tile(jnp.concatenate([-sin, sin], axis=1), (1, N_HEADS))
    return cos_t, sin_t


def _prepare(l, seqs, norm_mix, w_in, rg_conv_w, rg_conv_b, rg_wa, rg_ba, rg_wx, rg_bx, rg_lambda,
             hg_lb_logits, hg_norm_w, ssd_conv_w, ssd_conv_b, ssd_dt_bias, ssd_a_log, ssd_d,
             ssd_norm_w, w_out, norm_ffn, router_w, exp_w_gate, exp_w_up, exp_w_down):
    w = w_in[l]
    a_end = COLS_A
    b_end = a_end + COLS_B
    z0 = b_end
    x0 = z0 + GROUP_W
    dt0 = x0 + SSD_XBC
    d0 = dt0 + 2 * N_HEADS
    w_pad = jnp.concatenate(
        [w[:, :b_end], w[:, x0:dt0], w[:, z0:x0], w[:, dt0:d0],
         jnp.zeros((D_MODEL, LANES - 2 * N_HEADS), w.dtype), w[:, d0:]], axis=1).astype(BF16)
    row = lambda v: v.reshape(1, -1).astype(F32)
    rep = lambda v: jnp.repeat(v, HEAD_DIM).reshape(1, GROUP_W).astype(F32)
    expand = []
    for d in range(2):
        ex = np.zeros((LANES, GROUP_W), np.float32)
        for h in range(N_HEADS):
            ex[d * N_HEADS + h, h * HEAD_DIM:(h + 1) * HEAD_DIM] = 1.0
        expand.append(jnp.asarray(ex))
    bd_ones = np.kron(np.eye(N_HEADS, dtype=np.float32), np.ones((HEAD_DIM, HEAD_DIM), np.float32))
    ri, ci = np.indices((MIX_BLOCK, MIX_BLOCK))
    same_chunk = (ri // HG_CHUNK) == (ci // HG_CHUNK)
    tri_block = [jnp.asarray(m.astype(np.float32), BF16) for m in (ri >= ci, ri <= ci)]
    tri_chunk = [jnp.asarray((m & same_chunk).astype(np.float32), BF16)
                 for m in (ri >= ci, ri <= ci)]
    return {
        "norm_mix": row(norm_mix[l]), "w_pad": w_pad,
        "rg_conv_w": rg_conv_w[l], "rg_conv_b": row(rg_conv_b[l]),
        "rg_wg": [jnp.concatenate([_block_diag(rg_wa[l, d]), _block_diag(rg_wx[l, d])],
                                  axis=1).astype(BF16) for d in range(2)],
        "rg_bg": [jnp.concatenate([rg_ba[l, d], rg_bx[l, d]]).reshape(1, -1) for d in range(2)],
        "rg_lam": [row(rg_lambda[l, d]) for d in range(2)],
        "hg_lbl": [hg_lb_logits[:, d, :] for d in range(2)], "hg_norm_w": row(hg_norm_w[l]),
        "bd_f32": jnp.asarray(bd_ones), "bd_bf16": jnp.asarray(bd_ones, BF16),
        "tri_block": tri_block, "tri_chunk": tri_chunk,
        "ssd_conv_w": ssd_conv_w[l], "ssd_conv_b": row(ssd_conv_b[l]),
        "ssd_expand": [ex.astype(BF16) for ex in expand],
        "ssd_dt_bias": [rep(ssd_dt_bias[l, d]) for d in range(2)],
        "ssd_a_log": [rep(ssd_a_log[l, d]) for d in range(2)],
        "ssd_d": rep(ssd_d[l]), "ssd_norm_w": row(ssd_norm_w[l]),
        "rope": {s: _rope_tables(s) for s in seqs},
        "w_out": w_out[l].astype(BF16), "norm_ffn": row(norm_ffn[l]),
        "router_w": router_w[l].T,
        "wg": exp_w_gate[l].astype(BF16), "wu": exp_w_up[l].astype(BF16),
        "wd": exp_w_down[l].astype(BF16),
    }


def _trunk(x3, layers, norm_final):
    b, s, _ = x3.shape
    t = b * s
    x = x3.reshape(t, D_MODEL)
    delta = None
    for l, p in enumerate(layers):
        x, (ua, ub, uc, ud) = _inproj(x, delta, p["norm_mix"], p["w_pad"])
        shape3 = lambda u: u.reshape(b, s, u.shape[-1])
        mix = (_rglru(shape3(ua), p, b, s), _hgrn(shape3(ub), p, l, b, s),
               _ssd(shape3(uc), p, b, s), _ret(shape3(ud), p, b, s))
        mix = tuple(m.reshape(t, GROUP_W) for m in mix)
        x, h, aff_t = _outproj(mix, x, p["w_out"], p["norm_ffn"], p["router_w"])
        delta = _expert_choice(h, aff_t, p["wg"], p["wu"], p["wd"])
    return _final(x, delta, norm_final.reshape(1, -1)).reshape(b, s, D_MODEL)


def kernel(x_prompt, x_sample, norm_mix, w_in, rg_conv_w, rg_conv_b, rg_wa, rg_ba, rg_wx, rg_bx, rg_lambda, hg_lb_logits, hg_norm_w, ssd_conv_w, ssd_conv_b, ssd_dt_bias, ssd_a_log, ssd_d, ssd_norm_w, w_out, norm_ffn, router_w, exp_w_gate, exp_w_up, exp_w_down, norm_final):
    seqs = {x_prompt.shape[1], x_sample.shape[1]}
    layers = [_prepare(l, seqs, norm_mix, w_in, rg_conv_w, rg_conv_b, rg_wa, rg_ba, rg_wx, rg_bx,
                       rg_lambda, hg_lb_logits, hg_norm_w, ssd_conv_w, ssd_conv_b, ssd_dt_bias,
                       ssd_a_log, ssd_d, ssd_norm_w, w_out, norm_ffn, router_w, exp_w_gate,
                       exp_w_up, exp_w_down) for l in range(DEPTH)]
    return (_trunk(x_prompt, layers, norm_final), _trunk(x_sample, layers, norm_final))
```

```python
import functools
import math

import jax
import jax.numpy as jnp
import numpy as np
from jax import lax
from jax.experimental import pallas as pl
from jax.experimental.pallas import tpu as pltpu

F32 = jnp.float32
BF16 = jnp.bfloat16

D_MODEL = 1024
DEPTH = 2
GROUP_W = 256
N_HEADS = 4
HEAD_DIM = 64
EPS = 1e-6
RG_C = 8.0
SSD_STATE = 128
SSD_XBC = 768
N_EXPERTS = 16
EXPERT_FF = 2048
EC_CAPACITY_FACTOR = 2
RET_DECAY_EXP = (5.0, 5.5)
ROPE_BASE = 10000.0

LANES = 128
SUBLANES = 8
HALO = SUBLANES
MIX_BLOCK = 256
MIX_BATCH = 8
HG_CHUNK = 64
HG_SUB = 16
PROJ_ROWS = 512
EXP_ROWS = 128
EXP_TOKENS = 4096
VMEM_CAP = 64 * 1024 * 1024

COLS_A = 2 * GROUP_W
COLS_B = 5 * GROUP_W
COLS_C = SSD_XBC + GROUP_W + LANES
COLS_D = 4 * GROUP_W


def _sigmoid(x):
    return 1.0 / (1.0 + jnp.exp(-x))


def _silu(x):
    return x * _sigmoid(x)


def _softplus(x):
    return jnp.maximum(x, 0.0) + jnp.log(1.0 + jnp.exp(-jnp.abs(x)))


def _gelu_tanh(x):
    return 0.5 * x * (1.0 + jnp.tanh(math.sqrt(2.0 / math.pi) * (x + 0.044715 * (x * x * x))))


def _dot(a, b, precision=None):
    return jnp.dot(a, b, preferred_element_type=F32, precision=precision)


def _dot_nt(a, b, precision=None):
    return lax.dot_general(a, b, (((1,), (1,)), ((), ())), preferred_element_type=F32,
                           precision=precision)


def _dot_tn(a, b):
    return lax.dot_general(a, b, (((0,), (0,)), ((), ())), preferred_element_type=F32)


def _head_mask(h, width=GROUP_W, head_dim=HEAD_DIM):
    lane = lax.broadcasted_iota(jnp.int32, (1, width), 1)
    return (lane // head_dim == h).astype(F32).astype(BF16)


def _tri(n, reverse):
    r = lax.broadcasted_iota(jnp.int32, (n, n), 0)
    c = lax.broadcasted_iota(jnp.int32, (n, n), 1)
    return (r <= c) if reverse else (r >= c)


def _vmem(nbytes):
    return int(min(VMEM_CAP - (2 << 20), max(nbytes, 16 << 20)))


def _inproj_body(has_delta, *refs):
    if has_delta:
        x_ref, d_ref, nw_ref, w_ref, xo_ref, ua_ref, ub_ref, uc_ref, ud_ref = refs
        x = x_ref[...] + d_ref[...]
        xo_ref[...] = x
    else:
        x_ref, nw_ref, w_ref, ua_ref, ub_ref, uc_ref, ud_ref = refs
        x = x_ref[...]
    ms = jnp.mean(x * x, axis=-1, keepdims=True)
    h = (x * lax.rsqrt(ms + EPS) * nw_ref[...]).astype(BF16)
    c0 = 0
    for ref, n in ((ua_ref, COLS_A), (ub_ref, COLS_B), (uc_ref, COLS_C), (ud_ref, COLS_D)):
        ref[...] = _dot(h, w_ref[:, c0:c0 + n])
        c0 += n


def _inproj(x, delta, norm_w, w_pad):
    t = x.shape[0]
    tm = min(PROJ_ROWS, t)
    ncols = COLS_A + COLS_B + COLS_C + COLS_D
    row = lambda i: (i, 0)
    fixed = lambda i: (0, 0)
    xspec = pl.BlockSpec((tm, D_MODEL), row)
    in_specs = [xspec] + ([xspec] if delta is not None else []) + [
        pl.BlockSpec((1, D_MODEL), fixed), pl.BlockSpec((D_MODEL, ncols), fixed)]
    u_shapes = [jax.ShapeDtypeStruct((t, n), F32) for n in (COLS_A, COLS_B, COLS_C, COLS_D)]
    u_specs = [pl.BlockSpec((tm, n), row) for n in (COLS_A, COLS_B, COLS_C, COLS_D)]
    out_shape = ([jax.ShapeDtypeStruct((t, D_MODEL), F32)] if delta is not None else []) + u_shapes
    out_specs = ([xspec] if delta is not None else []) + u_specs
    args = (x,) + ((delta,) if delta is not None else ()) + (norm_w, w_pad)
    vm = 2 * (2 * tm * D_MODEL * 4 * 2 + D_MODEL * ncols * 2 + tm * ncols * 4) + (8 << 20)
    outs = pl.pallas_call(
        functools.partial(_inproj_body, delta is not None),
        grid=(t // tm,), in_specs=in_specs, out_specs=out_specs, out_shape=out_shape,
        compiler_params=pltpu.CompilerParams(dimension_semantics=("arbitrary",),
                                             vmem_limit_bytes=_vmem(vm)),
    )(*args)
    if delta is not None:
        return outs[0], outs[1:]
    return x, outs


def _conv4(x, prev8, next8, w, bias, first, last):
    n = x.shape[0]
    pz = jnp.where(first, 0.0, prev8)
    nz = jnp.where(last, 0.0, next8)
    y = (w[0:1] * pltpu.roll(x, 2, 0) + w[1:2] * pltpu.roll(x, 1, 0) + w[2:3] * x
         + w[3:4] * pltpu.roll(x, n - 1, 0) + bias)
    row = lax.broadcasted_iota(jnp.int32, (HALO, x.shape[1]), 0)
    dz = pz - x[n - HALO:]
    dn = nz - x[:HALO]
    top = (jnp.where(row < 2, w[0:1] * pltpu.roll(dz, 2, 0), 0.0)
           + jnp.where(row < 1, w[1:2] * pltpu.roll(dz, 1, 0), 0.0))
    bot = jnp.where(row == HALO - 1, w[3:4] * pltpu.roll(dn, HALO - 1, 0), 0.0)
    return jnp.concatenate([y[:HALO] + top, y[HALO:n - HALO], y[n - HALO:] + bot], axis=0)


def _mixer_call(body, reverse, b, s, row_inputs, halo_inputs, const_inputs, table_inputs,
                out_dtype, state, tables=(), table_init=None, extra_row_inputs=(),
                extra_outputs=()):
    nblk = s // MIX_BLOCK
    hb = MIX_BLOCK // HALO
    nb = math.gcd(b, MIX_BATCH)

    def blk(j):
        return (nblk - 1 - j) if reverse else j

    in_specs, args, per_batch = [], [], []
    for arr, cb, w in tuple(row_inputs) + tuple(extra_row_inputs):
        in_specs.append(pl.BlockSpec((nb, MIX_BLOCK, w), lambda bi, j, cb=cb: (bi, blk(j), cb)))
        args.append(arr)
    for arr, cb, w in halo_inputs:
        in_specs.append(pl.BlockSpec((nb, MIX_BLOCK, w), lambda bi, j, cb=cb: (bi, blk(j), cb)))
        in_specs.append(pl.BlockSpec(
            (nb, HALO, w), lambda bi, j, cb=cb: (bi, jnp.maximum(blk(j) * hb - 1, 0), cb)))
        in_specs.append(pl.BlockSpec(
            (nb, HALO, w),
            lambda bi, j, cb=cb: (bi, jnp.minimum((blk(j) + 1) * hb, s // HALO - 1), cb)))
        args += [arr, arr, arr]
    per_batch += [True] * len(args)
    for arr in table_inputs:
        in_specs.append(pl.BlockSpec((MIX_BLOCK, arr.shape[1]), lambda bi, j: (blk(j), 0)))
        args.append(arr)
    for arr in const_inputs:
        in_specs.append(pl.BlockSpec(arr.shape, lambda bi, j, nd=arr.ndim: (0,) * nd))
        args.append(arr)
    outs = ((GROUP_W, out_dtype),) + tuple(extra_outputs)
    per_batch += [False] * (len(args) - len(per_batch)) + [True] * (len(outs) + len(state))
    per_batch += [False] * len(tables)

    def step(*refs):
        scratch_refs = refs[len(refs) - len(state) - len(tables):]

        @pl.when(pl.program_id(1) == 0)
        def _():
            for r in scratch_refs[:len(state)]:
                r[...] = jnp.zeros_like(r)

        if table_init is not None:
            @pl.when((pl.program_id(0) == 0) & (pl.program_id(1) == 0))
            def _():
                table_init(reverse, *scratch_refs[len(state):])

        for n in range(nb):
            body(reverse, nblk, *[r.at[n] if pb else r for r, pb in zip(refs, per_batch)])

    res = pl.pallas_call(
        step, grid=(b // nb, nblk), in_specs=in_specs,
        out_specs=[pl.BlockSpec((nb, MIX_BLOCK, w), lambda bi, j: (bi, blk(j), 0))
                   for w, _ in outs],
        out_shape=[jax.ShapeDtypeStruct((b, s, w), dt) for w, dt in outs],
        scratch_shapes=([pltpu.VMEM((nb,) + shape, dtype) for shape, dtype in state]
                        + [pltpu.VMEM(shape, dtype) for shape, dtype in tables]),
        compiler_params=pltpu.CompilerParams(dimension_semantics=("arbitrary", "arbitrary"),
                                             vmem_limit_bytes=_vmem(40 << 20)),
    )(*args)
    return res if extra_outputs else res[0]


def _edge_flags(reverse, nblk):
    j = pl.program_id(1)
    jj = (nblk - 1 - j) if reverse else j
    return jj == 0, jj == nblk - 1


def _linear_scan(a, b, carry, reverse):
    n = a.shape[0]
    row = lax.broadcasted_iota(jnp.int32, a.shape, 0) % SUBLANES
    d = 1
    while d < SUBLANES:
        shift = (n - d) if reverse else d
        m = (row < SUBLANES - d) if reverse else (row >= d)
        a_s = pltpu.roll(a, shift, 0)
        b_s = pltpu.roll(b, shift, 0)
        b = jnp.where(m, a * b_s + b, b)
        a = jnp.where(m, a * a_s, a)
        d *= 2
    ngroups = n // SUBLANES
    out = [None] * ngroups
    for g in (range(ngroups - 1, -1, -1) if reverse else range(ngroups)):
        sl = slice(g * SUBLANES, (g + 1) * SUBLANES)
        h = b[sl] + a[sl] * carry
        out[g] = h
        carry = h[0:1] if reverse else h[SUBLANES - 1:SUBLANES]
    return jnp.concatenate(out, axis=0), carry


def _rglru_body(reverse, nblk, *refs):
    if reverse:
        (x_ref, xp_ref, xn_ref, cw_ref, cb_ref, wg_ref, bg_ref, lam_ref, out_ref, u_ref,
         carry) = refs
        first, last = _edge_flags(reverse, nblk)
        u = _conv4(x_ref[...], xp_ref[...], xn_ref[...], cw_ref[...], cb_ref[...], first, last)
        u_ref[...] = u
    else:
        ga_ref, hb_ref, u_ref, wg_ref, bg_ref, lam_ref, out_ref, carry = refs
        u = u_ref[...]
    g = _dot(u.astype(BF16), wg_ref[...]) + bg_ref[...]
    r = _sigmoid(g[:, :GROUP_W])
    i = _sigmoid(g[:, GROUP_W:])
    log_a = -RG_C * r * _softplus(-lam_ref[...])
    a = jnp.exp(log_a)
    inp = jnp.sqrt(1.0 - jnp.exp(2.0 * log_a)) * i * u
    h, carry[...] = _linear_scan(a, inp, carry[...], reverse)
    if reverse:
        out_ref[...] = h
    else:
        out_ref[...] = ((h + hb_ref[...]) * _gelu_tanh(ga_ref[...])).astype(out_ref.dtype)


def _rglru(ua, p, b, s):
    scratch = [((1, GROUP_W), F32)]
    gates = [(p["rg_wg"][d], p["rg_bg"][d], p["rg_lam"][d]) for d in range(2)]
    hb, u = _mixer_call(_rglru_body, True, b, s, (), ((ua, 0, GROUP_W),),
                        (p["rg_conv_w"], p["rg_conv_b"]) + gates[1], (), F32, scratch,
                        extra_outputs=((GROUP_W, F32),))
    return _mixer_call(_rglru_body, False, b, s,
                       ((ua, 1, GROUP_W), (hb, 0, GROUP_W), (u, 0, GROUP_W)), (), gates[0], (),
                       BF16, scratch)


def _split3(x):
    x0 = x.astype(BF16)
    r1 = x - x0.astype(F32)
    x1 = r1.astype(BF16)
    return x0, x1, (r1 - x1.astype(F32)).astype(BF16)


def _cumsum_rows(tri, x):
    return functools.reduce(lambda a, b: a + b, [_dot(tri, piece) for piece in _split3(x)])


def _select_cols(x, sel):
    return functools.reduce(lambda a, b: a + b, [_dot(piece, sel) for piece in _split3(x)])


def _head_mean_sq(o, bd):
    sq = o * o
    hi = sq.astype(BF16)
    lo = (sq - hi.astype(F32)).astype(BF16)
    return (_dot(hi, bd) + _dot(lo, bd)) * (1.0 / HEAD_DIM)


def _gla_chunk(q, k, v, cum, st, bd, reverse):
    c, nsub = HG_CHUNK, HG_CHUNK // HG_SUB
    cum_last = cum[0:1] if reverse else cum[c - 1:c]
    masks = [_head_mask(h) for h in range(N_HEADS)]
    order = list(range(nsub - 1, -1, -1)) if reverse else list(range(nsub))
    entry = {}
    for n_done, i in enumerate(order):
        r0 = i * HG_SUB
        if n_done == 0:
            entry[i] = jnp.zeros((1, GROUP_W), F32)
        else:
            entry[i] = cum[r0 + HG_SUB:r0 + HG_SUB + 1] if reverse else cum[r0 - 1:r0]
    entry_rows = jnp.concatenate(
        [jnp.broadcast_to(entry[i], (HG_SUB, GROUP_W)) for i in range(nsub)], axis=0)
    qh = q * jnp.exp(cum - entry_rows)
    kh = k * jnp.exp(entry_rows - cum)
    khb = kh.astype(BF16)
    kh_bd = jnp.concatenate([khb * m for m in masks], axis=0)
    pairs, lhs = [], []
    for n_done, i in enumerate(order):
        qi = qh[i * HG_SUB:(i + 1) * HG_SUB]
        for j in order[:n_done + 1]:
            pairs.append((i, j))
            lhs.append((qi if j == i else qi * jnp.exp(entry[i] - entry[j])).astype(BF16))
    res = _dot_nt(jnp.concatenate(lhs, axis=0), kh_bd)
    scol = lax.broadcasted_iota(jnp.int32, (HG_SUB, GROUP_W), 1) % HEAD_DIM
    trow = lax.broadcasted_iota(jnp.int32, (HG_SUB, GROUP_W), 0)
    blocks = [None] * nsub
    for n, (i, j) in enumerate(pairs):
        keep = scol // HG_SUB == j
        if i == j:
            keep = keep & ((scol >= trow + i * HG_SUB) if reverse else (scol <= trow + i * HG_SUB))
        piece = jnp.where(keep, res[n * HG_SUB:(n + 1) * HG_SUB], 0.0)
        blocks[i] = piece if blocks[i] is None else blocks[i] + piece
    scores = jnp.concatenate(blocks, axis=0).astype(BF16)
    vb = v.astype(BF16)
    v_bd = jnp.concatenate([vb * m for m in masks], axis=0)
    y = _dot(scores, v_bd)
    y = y + _dot_nt((q * jnp.exp(cum)).astype(BF16), st.astype(BF16))
    kw = (k * jnp.exp(cum_last - cum)).astype(BF16)
    st = st * jnp.exp(cum_last) + bd * _dot_tn(vb, kw)
    return y, st


def _hgrn_body(reverse, nblk, *refs, layer):
    if reverse:
        q_ref, f_ref, i_ref, lbl_ref, tri_ref, bd_ref, out_ref, st_ref = refs
    else:
        (q_ref, f_ref, i_ref, g_ref, ob_ref, lbl_ref, tri_ref, bd_ref, nw_ref, bdb_ref, out_ref,
         st_ref) = refs
    rows = [lbl_ref[r:r + 1, :] for r in range(DEPTH)]
    mx = functools.reduce(jnp.maximum, rows)
    es = [jnp.exp(r - mx) for r in rows]
    tot = functools.reduce(lambda x, y: x + y, es)
    sm = [e / tot for e in es]
    lb = functools.reduce(lambda x, y: x + y, sm[:layer + 1]) - sm[0]

    f = lb + (1.0 - lb) * _sigmoid(f_ref[...])
    lf = jnp.log(f)
    k = 1.0 - f
    q = q_ref[...]
    v = i_ref[...]
    nchunk = MIX_BLOCK // HG_CHUNK
    cum = _cumsum_rows(tri_ref[...], lf)
    bd = bd_ref[...]
    st = st_ref[...]
    ys = [None] * nchunk
    for c in (range(nchunk - 1, -1, -1) if reverse else range(nchunk)):
        sl = slice(c * HG_CHUNK, (c + 1) * HG_CHUNK)
        ys[c], st = _gla_chunk(q[sl], k[sl], v[sl], cum[sl], st, bd, reverse)
    st_ref[...] = st
    y = jnp.concatenate(ys, axis=0)
    if reverse:
        out_ref[...] = y
    else:
        o = y + ob_ref[...]
        o = o * lax.rsqrt(_head_mean_sq(o, bdb_ref[...]) + EPS) * nw_ref[...]
        out_ref[...] = (o * _silu(g_ref[...])).astype(out_ref.dtype)


def _hgrn(ub, p, layer, b, s):
    scratch = [((GROUP_W, GROUP_W), F32)]
    ob = _mixer_call(functools.partial(_hgrn_body, layer=layer), True, b, s,
                     ((ub, 0, GROUP_W), (ub, 2, GROUP_W), (ub, 3, GROUP_W)), (),
                     (p["hg_lbl"][1], p["tri_chunk"][1], p["bd_f32"]), (), F32, scratch)
    return _mixer_call(functools.partial(_hgrn_body, layer=layer), False, b, s,
                       ((ub, 0, GROUP_W), (ub, 1, GROUP_W), (ub, 3, GROUP_W), (ub, 4, GROUP_W),
                        (ob, 0, GROUP_W)), (),
                       (p["hg_lbl"][0], p["tri_chunk"][0], p["bd_f32"], p["hg_norm_w"],
                        p["bd_bf16"]), (), BF16, scratch)


def _ssd_body(reverse, nblk, *refs):
    n = MIX_BLOCK
    if reverse:
        (dt_ref, x_ref, xp_ref, xn_ref, cw_ref, cb_ref, ex_ref, dtb_ref, alog_ref, tri_ref,
         out_ref, xs_ref, bc_ref, st_ref) = refs
        first, last = _edge_flags(reverse, nblk)
        xbc = _silu(_conv4(x_ref[...], xp_ref[...], xn_ref[...], cw_ref[...], cb_ref[...],
                           first, last))
        xs = xbc[:, :GROUP_W]
        bm = xbc[:, GROUP_W:2 * GROUP_W].astype(BF16)
        cm = xbc[:, 2 * GROUP_W:].astype(BF16)
        xs_ref[...] = xs
        bc_ref[:, :GROUP_W] = bm
        bc_ref[:, GROUP_W:] = cm
    else:
        (dt_ref, z_ref, yb_ref, xs_ref, bc_ref, ex_ref, dtb_ref, alog_ref, tri_ref, dsk_ref, nw_ref,
         out_ref, st_ref) = refs
        xs = xs_ref[...]
        bm = bc_ref[:, :GROUP_W]
        cm = bc_ref[:, GROUP_W:]
    dt = _softplus(_select_cols(dt_ref[...], ex_ref[...]) + dtb_ref[...])
    la = dt * (-jnp.exp(alog_ref[...]))
    cum = _cumsum_rows(tri_ref[...], la)
    cum_last = cum[0:1] if reverse else cum[n - 1:n]
    cum_t = cum.T
    v = xs * dt
    vb = v.astype(BF16)
    keep = _tri(n, reverse)
    y = jnp.zeros((n, GROUP_W), F32)
    sc = [_dot_nt(cm[:, g * SSD_STATE:(g + 1) * SSD_STATE],
                  bm[:, g * SSD_STATE:(g + 1) * SSD_STATE]) for g in range(2)]
    for h in range(N_HEADS):
        l0 = h * HEAD_DIM
        seg = cum[:, l0:l0 + 1] - cum_t[l0:l0 + 1, :]
        dec = jnp.exp(jnp.where(keep, seg, -jnp.inf))
        y = y + _dot((sc[h // 2] * dec).astype(BF16), vb * _head_mask(h))
    st = st_ref[...]
    inter = jnp.concatenate(
        [_dot(cm[:, g * SSD_STATE:(g + 1) * SSD_STATE],
              st[:, g * SSD_STATE:(g + 1) * SSD_STATE].astype(BF16)) for g in range(2)], axis=1)
    y = y + jnp.exp(cum) * inter
    vw = (v * jnp.exp(cum_last - cum)).astype(BF16)
    upd = jnp.concatenate(
        [_dot_tn(bm[:, g * SSD_STATE:(g + 1) * SSD_STATE],
                 vw[:, g * SSD_STATE:(g + 1) * SSD_STATE]) for g in range(2)], axis=1)
    st_ref[...] = st * jnp.exp(cum_last) + upd
    if reverse:
        out_ref[...] = y
    else:
        y = (y + yb_ref[...] + dsk_ref[...] * xs) * _silu(z_ref[...])
        ms = jnp.mean(y * y, axis=-1, keepdims=True)
        out_ref[...] = (y * lax.rsqrt(ms + EPS) * nw_ref[...]).astype(out_ref.dtype)


def _ssd(uc, p, b, s):
    scratch = [((SSD_STATE, GROUP_W), F32)]
    dt_cb = (SSD_XBC + GROUP_W) // LANES
    z_cb = SSD_XBC // GROUP_W
    dirs = [(p["ssd_expand"][d], p["ssd_dt_bias"][d], p["ssd_a_log"][d], p["tri_block"][d])
            for d in range(2)]
    yb, xs, bc = _mixer_call(_ssd_body, True, b, s, ((uc, dt_cb, LANES),), ((uc, 0, SSD_XBC),),
                             (p["ssd_conv_w"], p["ssd_conv_b"]) + dirs[1], (), F32, scratch,
                             extra_outputs=((GROUP_W, F32), (2 * GROUP_W, BF16)))
    return _mixer_call(_ssd_body, False, b, s,
                       ((uc, dt_cb, LANES), (uc, z_cb, GROUP_W), (yb, 0, GROUP_W),
                        (xs, 0, GROUP_W), (bc, 0, 2 * GROUP_W)), (),
                       dirs[0] + (p["ssd_d"], p["ssd_norm_w"]), (), BF16, scratch)


def _ret_log_decay(reverse):
    exp0 = RET_DECAY_EXP[1] if reverse else RET_DECAY_EXP[0]
    return [math.log1p(-2.0 ** (-exp0 - h)) for h in range(N_HEADS)]


def _ret_decay_table(reverse, dec_ref):
    n = MIX_BLOCK
    r = lax.broadcasted_iota(jnp.int32, (n, n), 0)
    c = lax.broadcasted_iota(jnp.int32, (n, n), 1)
    dist = (c - r) if reverse else (r - c)
    keep = dist >= 0
    distf = jnp.where(keep, dist, 0).astype(F32)
    for h, lg in enumerate(_ret_log_decay(reverse)):
        dec_ref[h] = jnp.where(keep, jnp.exp(distf * lg), 0.0)


def _ret_body(reverse, nblk, *refs):
    if reverse:
        q_ref, k_ref, v_ref, cos_ref, sin_ref, bd_ref, out_ref, st_ref, dec_ref = refs
    else:
        (q_ref, k_ref, v_ref, g_ref, ob_ref, cos_ref, sin_ref, bd_ref, bdb_ref, out_ref, st_ref,
         dec_ref) = refs
    n = MIX_BLOCK
    log_g = _ret_log_decay(reverse)
    lane = lax.broadcasted_iota(jnp.int32, (n, GROUP_W), 1)
    low_half = (lane % HEAD_DIM) < HEAD_DIM // 2
    cos = cos_ref[...]
    sin = sin_ref[...]

    def rot(x):
        swapped = jnp.where(low_half, pltpu.roll(x, GROUP_W - HEAD_DIM // 2, 1),
                            pltpu.roll(x, HEAD_DIM // 2, 1))
        return x * cos + swapped * sin

    lane_row = lax.broadcasted_iota(jnp.int32, (1, GROUP_W), 1) // HEAD_DIM
    la = functools.reduce(lambda acc, h: jnp.where(lane_row == h, log_g[h], acc),
                          range(N_HEADS), jnp.zeros((1, GROUP_W), F32))
    rowi = lax.broadcasted_iota(jnp.int32, (n, 1), 0)
    steps = ((n - rowi) if reverse else (rowi + 1)).astype(F32)
    cum = steps * la
    cum_last = float(n) * la
    qr = rot(q_ref[...])
    kr = rot(k_ref[...]) * (HEAD_DIM ** -0.5)
    v = v_ref[...]
    qb = qr.astype(BF16)
    kb = kr.astype(BF16)
    vb = v.astype(BF16)
    y = jnp.zeros((n, GROUP_W), F32)
    for h in range(N_HEADS):
        m = _head_mask(h)
        sc = _dot_nt(qb * m, kb)
        y = y + _dot((sc * dec_ref[h]).astype(BF16), vb * m)
    st = st_ref[...]
    y = y + _dot_nt((qr * jnp.exp(cum)).astype(BF16), st.astype(BF16))
    kw = (kr * jnp.exp(cum_last - cum)).astype(BF16)
    st_ref[...] = st * jnp.exp(cum_last) + bd_ref[...] * _dot_tn(vb, kw)
    if reverse:
        out_ref[...] = y
    else:
        o = y + ob_ref[...]
        ms = _head_mean_sq(o, bdb_ref[...])
        out_ref[...] = (o * lax.rsqrt(ms + EPS) * _silu(g_ref[...])).astype(out_ref.dtype)


def _ret(ud, p, b, s):
    state = [((GROUP_W, GROUP_W), F32)]
    tables = [((N_HEADS, MIX_BLOCK, MIX_BLOCK), F32)]
    cos, sin = p["rope"][s]
    ob = _mixer_call(_ret_body, True, b, s,
                     ((ud, 0, GROUP_W), (ud, 1, GROUP_W), (ud, 2, GROUP_W)), (), (p["bd_f32"],),
                     (cos, sin), F32, state, tables, _ret_decay_table)
    return _mixer_call(_ret_body, False, b, s,
                       ((ud, 0, GROUP_W), (ud, 1, GROUP_W), (ud, 2, GROUP_W), (ud, 3, GROUP_W),
                        (ob, 0, GROUP_W)), (), (p["bd_f32"], p["bd_bf16"]), (cos, sin), BF16,
                       state, tables, _ret_decay_table)


def _outproj_body(oa_ref, ob_ref, oc_ref, od_ref, x_ref, w_ref, nw_ref, rw_ref,
                  xo_ref, h_ref, aff_ref):
    acc = x_ref[...]
    for n, ref in enumerate((oa_ref, ob_ref, oc_ref, od_ref)):
        acc = acc + _dot(ref[...], w_ref[n * GROUP_W:(n + 1) * GROUP_W, :])
    xo_ref[...] = acc
    ms = jnp.mean(acc * acc, axis=-1, keepdims=True)
    h = acc * lax.rsqrt(ms + EPS) * nw_ref[...]
    bits = pltpu.bitcast(h.astype(BF16).astype(F32), jnp.uint32)
    half = D_MODEL // 2
    h_ref[...] = (bits[:, :half] >> 16) | (bits[:, half:] & jnp.uint32(0xFFFF0000))
    h0 = h.astype(BF16)
    h1 = (h - h0.astype(F32)).astype(BF16)
    rw = rw_ref[...]
    r0 = rw.astype(BF16)
    r1 = (rw - r0.astype(F32)).astype(BF16)
    logits = _dot_nt(r0, h0) + _dot_nt(r0, h1) + _dot_nt(r1, h0)
    e = jnp.exp(logits - jnp.max(logits, axis=0, keepdims=True))
    aff_ref[...] = e / jnp.sum(e, axis=0, keepdims=True)


def _outproj(mix, x, w_out, norm_w, router_w):
    t = x.shape[0]
    tm = min(PROJ_ROWS, t)
    row = lambda i: (i, 0)
    fixed = lambda i: (0, 0)
    in_specs = [pl.BlockSpec((tm, GROUP_W), row)] * 4 + [
        pl.BlockSpec((tm, D_MODEL), row), pl.BlockSpec((D_MODEL, D_MODEL), fixed),
        pl.BlockSpec((1, D_MODEL), fixed), pl.BlockSpec((N_EXPERTS, D_MODEL), fixed)]
    out_specs = [pl.BlockSpec((tm, D_MODEL), row), pl.BlockSpec((tm, D_MODEL // 2), row),
                 pl.BlockSpec((N_EXPERTS, tm), lambda i: (0, i))]
    out_shape = [jax.ShapeDtypeStruct((t, D_MODEL), F32),
                 jax.ShapeDtypeStruct((t, D_MODEL // 2), jnp.uint32),
                 jax.ShapeDtypeStruct((N_EXPERTS, t), F32)]
    return pl.pallas_call(
        _outproj_body, grid=(t // tm,), in_specs=in_specs, out_specs=out_specs,
        out_shape=out_shape,
        compiler_params=pltpu.CompilerParams(dimension_semantics=("arbitrary",),
                                             vmem_limit_bytes=_vmem(40 << 20)),
    )(*mix, x, w_out, norm_w, router_w)


def _route_body(cap, aff_ref, pos_ref, off_ref):
    a = aff_ref[...]
    ng = a.shape[0]
    bits = pltpu.bitcast(a, jnp.int32)

    def count(m):
        return jnp.sum(jnp.sum(m.astype(F32), axis=0, keepdims=True), axis=1, keepdims=True)

    def step(k, prefix):
        cand = prefix | jnp.left_shift(jnp.int32(1), 30 - k)
        return jnp.where(count(bits >= cand) >= cap, cand, prefix)

    thr = lax.fori_loop(0, 31, step, jnp.zeros((1, 1), jnp.int32))
    r = lax.broadcasted_iota(jnp.int32, (LANES, LANES), 0)
    c = lax.broadcasted_iota(jnp.int32, (LANES, LANES), 1)
    incl = (r <= c).astype(BF16)
    ones = jnp.ones((LANES, LANES), BF16)
    gr = lax.broadcasted_iota(jnp.int32, (ng, ng), 0)
    gc = lax.broadcasted_iota(jnp.int32, (ng, ng), 1)
    before = (gc < gr).astype(BF16)

    def prefix(m):
        mb = m.astype(BF16)
        group_off = _dot(before, _dot(mb, ones).astype(BF16))
        return _dot(mb, incl) - m.astype(F32) + group_off, group_off

    gt = bits > thr
    eq = bits == thr
    need = cap - count(gt)
    eq_rank, _ = prefix(eq)
    sel = gt | (eq & (eq_rank < need))
    pos, group_off = prefix(sel)
    pos_ref[...] = jnp.where(sel, pos, -1.0).astype(jnp.int32)
    off_ref[...] = group_off.astype(jnp.int32)


def _route(aff3, cap):
    ne, ng, _ = aff3.shape
    spec = pl.BlockSpec((None, ng, LANES), lambda e: (e, 0, 0))
    return pl.pallas_call(
        functools.partial(_route_body, cap), grid=(ne,), in_specs=[spec], out_specs=[spec, spec],
        out_shape=[jax.ShapeDtypeStruct(aff3.shape, jnp.int32)] * 2,
        compiler_params=pltpu.CompilerParams(dimension_semantics=("arbitrary",),
                                             vmem_limit_bytes=_vmem(32 << 20)),
    )(aff3)


def _tile_span(off_ref, gpt, ngroups):
    obase = pl.program_id(1) * (ngroups + 1) + pl.program_id(0) * gpt
    base = off_ref[obase]
    return base, off_ref[obase + gpt] - base, obase


def _lists_body(gpt, ngroups, off_ref, pos_ref, idx_ref):
    base, _, obase = _tile_span(off_ref, gpt, ngroups)
    nsub = idx_ref.shape[0]
    pos_t = pos_ref[...].astype(F32).T
    tok = lax.broadcasted_iota(jnp.int32, (LANES, EXP_ROWS), 0).astype(F32)
    row = lax.broadcasted_iota(jnp.int32, (1, EXP_ROWS), 1)
    idx_ref[...] = jnp.zeros_like(idx_ref)
    for g in range(gpt):
        first = (off_ref[obase + g] - base) // EXP_ROWS
        pos_g = jnp.broadcast_to(pos_t[:, g:g + 1], (LANES, EXP_ROWS))
        for s in (first, first + 1):
            hit = pos_g == (row + (base + s * EXP_ROWS)).astype(F32)
            add = jnp.sum(jnp.where(hit, tok + float(g * LANES), 0.0), axis=0, keepdims=True)
            idx_ref[jnp.minimum(s, nsub - 1)] += add.astype(jnp.int32)


def _lists(pos, offsets, tt):
    ne, ngroups, _ = pos.shape
    gpt = tt // LANES
    ntiles = ngroups // gpt
    nsub = tt // EXP_ROWS
    shape = (ne, ntiles, nsub, 1, EXP_ROWS)
    grid_spec = pltpu.PrefetchScalarGridSpec(
        num_scalar_prefetch=1, grid=(ntiles, ne),
        in_specs=[pl.BlockSpec((None, gpt, LANES), lambda i, e, off: (e, i, 0))],
        out_specs=pl.BlockSpec((None, None, nsub, 1, EXP_ROWS), lambda i, e, off: (e, i, 0, 0, 0)))
    idx = pl.pallas_call(
        functools.partial(_lists_body, gpt, ngroups), grid_spec=grid_spec,
        out_shape=jax.ShapeDtypeStruct(shape, jnp.int32),
        compiler_params=pltpu.CompilerParams(dimension_semantics=("arbitrary", "arbitrary")),
    )(offsets, pos)
    return idx.reshape(ne, ntiles, 1, tt)


SCATTER_UNROLL = 4


def _expert_body(gpt, ngroups, off_ref, h_ref, idx_ref, gate_ref, wg_ref, wu_ref, wd_ref, y_acc,
                 xe, out):
    @pl.when(pl.program_id(1) == 0)
    def _():
        y_acc[...] = jnp.zeros_like(y_acc)

    @pl.when((pl.program_id(0) == 0) & (pl.program_id(1) == 0))
    def _():
        out[...] = jnp.zeros_like(out)

    _, cnt, _ = _tile_span(off_ref, gpt, ngroups)
    nsub = (cnt + EXP_ROWS - 1) // EXP_ROWS
    last_block = idx_ref.shape[1] // EXP_ROWS - 1

    def gather(dst, s):
        r0 = jnp.minimum(s, last_block) * EXP_ROWS
        for r in range(EXP_ROWS):
            dst[pl.ds(r, 1), :] = h_ref[pl.ds(idx_ref[0, r0 + r], 1), :]

    def ffn(src, dst):
        w = src[...]
        x = jnp.concatenate([pltpu.bitcast(w << 16, F32),
                             pltpu.bitcast(w & jnp.uint32(0xFFFF0000), F32)], axis=1).astype(BF16)
        he = (_silu(_dot(x, wg_ref[...])) * _dot(x, wu_ref[...])).astype(BF16)
        dst[...] = _dot(he, wd_ref[...])

    def scatter(src, s):
        live = s >= 0
        r0 = jnp.maximum(s, 0) * EXP_ROWS
        for k in range(0, EXP_ROWS, SCATTER_UNROLL):
            rows = [k + u for u in range(SCATTER_UNROLL)]
            toks = [idx_ref[0, r0 + r] for r in rows]
            gates = [jnp.where(live & (r0 + r < cnt), gate_ref[0, t], 0.0)
                     for r, t in zip(rows, toks)]
            vals = [y_acc[pl.ds(t, 1), :] + g * src[pl.ds(r, 1), :]
                    for r, t, g in zip(rows, toks, gates)]
            for t, v in reversed(list(zip(toks, vals))):
                y_acc[pl.ds(t, 1), :] = v

    @pl.when(nsub > 0)
    def _():
        gather(xe.at[0], 0)

    def sub_tile(s, carry):
        slot = s & 1
        ffn(xe.at[slot], out.at[slot])
        gather(xe.at[1 - slot], s + 1)
        scatter(out.at[1 - slot], s - 1)
        return carry

    lax.fori_loop(0, nsub, sub_tile, 0)

    @pl.when(nsub > 0)
    def _():
        scatter(out.at[(nsub - 1) & 1], nsub - 1)


def _experts(h, idx, gates, offsets, wg, wu, wd, tt):
    t = h.shape[0]
    gpt = tt // LANES
    ngroups = t // LANES
    half = D_MODEL // 2
    smem_list = pl.BlockSpec((None, None, 1, tt), lambda i, e, off: (e, i, 0, 0),
                             memory_space=pltpu.SMEM)
    grid_spec = pltpu.PrefetchScalarGridSpec(
        num_scalar_prefetch=1, grid=(t // tt, N_EXPERTS),
        in_specs=[
            pl.BlockSpec((tt, half), lambda i, e, off: (i, 0), pipeline_mode=pl.Buffered(1)),
            smem_list, smem_list,
            pl.BlockSpec((None, D_MODEL, EXPERT_FF), lambda i, e, off: (e, 0, 0)),
            pl.BlockSpec((None, D_MODEL, EXPERT_FF), lambda i, e, off: (e, 0, 0)),
            pl.BlockSpec((None, EXPERT_FF, D_MODEL), lambda i, e, off: (e, 0, 0)),
        ],
        out_specs=pl.BlockSpec((tt, D_MODEL), lambda i, e, off: (i, 0),
                               pipeline_mode=pl.Buffered(1)),
        scratch_shapes=[pltpu.VMEM((2, EXP_ROWS, half), jnp.uint32),
                        pltpu.VMEM((2, EXP_ROWS, D_MODEL), F32)],
    )
    vm = (tt * D_MODEL * (2 + 4) + 2 * 3 * D_MODEL * EXPERT_FF * 2
          + EXP_ROWS * (3 * EXPERT_FF + 4 * D_MODEL) * 4 + (4 << 20))
    return pl.pallas_call(
        functools.partial(_expert_body, gpt, ngroups), grid_spec=grid_spec,
        out_shape=jax.ShapeDtypeStruct((t, D_MODEL), F32),
        compiler_params=pltpu.CompilerParams(dimension_semantics=("arbitrary", "arbitrary"),
                                             vmem_limit_bytes=_vmem(vm)),
    )(offsets, h, idx, gates, wg, wu, wd)


def _expert_choice(h, aff_t, wg, wu, wd):
    t = h.shape[0]
    tt = min(EXP_TOKENS, t)
    cap = (EC_CAPACITY_FACTOR * t) // N_EXPERTS
    aff3 = aff_t.reshape(N_EXPERTS, t // LANES, LANES)
    pos, off = _route(aff3, cap)
    offsets = jnp.concatenate([off[:, :, 0], jnp.full((N_EXPERTS, 1), cap, jnp.int32)],
                              axis=1).reshape(-1)
    idx = _lists(pos, offsets, tt)
    gates = aff_t.reshape(N_EXPERTS, t // tt, 1, tt)
    return _experts(h, idx, gates, offsets, wg, wu, wd, tt)


def _final_body(x_ref, d_ref, nw_ref, o_ref):
    x = x_ref[...] + d_ref[...]
    ms = jnp.mean(x * x, axis=-1, keepdims=True)
    o_ref[...] = x * lax.rsqrt(ms + EPS) * nw_ref[...]


def _final(x, delta, norm_w):
    t = x.shape[0]
    tm = min(PROJ_ROWS, t)
    spec = pl.BlockSpec((tm, D_MODEL), lambda i: (i, 0))
    return pl.pallas_call(
        _final_body, grid=(t // tm,),
        in_specs=[spec, spec, pl.BlockSpec((1, D_MODEL), lambda i: (0, 0))], out_specs=spec,
        out_shape=jax.ShapeDtypeStruct((t, D_MODEL), F32),
        compiler_params=pltpu.CompilerParams(dimension_semantics=("arbitrary",)),
    )(x, delta, norm_w)


def _block_diag(w):
    nb, k, j = w.shape
    out = jnp.zeros((nb * k, nb * j), w.dtype)
    for n in range(nb):
        out = out.at[n * k:(n + 1) * k, n * j:(n + 1) * j].set(w[n])
    return out


def _rope_tables(s):
    half = HEAD_DIM // 2
    inv_freq = ROPE_BASE ** (-jnp.arange(half, dtype=F32) / half)
    ang = jnp.arange(s, dtype=F32)[:, None] * inv_freq[None, :]
    cos, sin = jnp.cos(ang), jnp.sin(ang)
    cos_t = jnp.tile(jnp.concatenate([cos, cos], axis=1), (1, N_HEADS))
    sin_t = jnp.tile(jnp.concatenate([-sin, sin], axis=1), (1, N_HEADS))
    return cos_t, sin_t


def _prepare(l, seqs, norm_mix, w_in, rg_conv_w, rg_conv_b, rg_wa, rg_ba, rg_wx, rg_bx, rg_lambda,
             hg_lb_logits, hg_norm_w, ssd_conv_w, ssd_conv_b, ssd_dt_bias, ssd_a_log, ssd_d,
             ssd_norm_w, w_out, norm_ffn, router_w, exp_w_gate, exp_w_up, exp_w_down):
    w = w_in[l]
    a_end = COLS_A
    b_end = a_end + COLS_B
    z0 = b_end
    x0 = z0 + GROUP_W
    dt0 = x0 + SSD_XBC
    d0 = dt0 + 2 * N_HEADS
    w_pad = jnp.concatenate(
        [w[:, :b_end], w[:, x0:dt0], w[:, z0:x0], w[:, dt0:d0],
         jnp.zeros((D_MODEL, LANES - 2 * N_HEADS), w.dtype), w[:, d0:]], axis=1).astype(BF16)
    row = lambda v: v.reshape(1, -1).astype(F32)
    rep = lambda v: jnp.repeat(v, HEAD_DIM).reshape(1, GROUP_W).astype(F32)
    expand = []
    for d in range(2):
        ex = np.zeros((LANES, GROUP_W), np.float32)
        for h in range(N_HEADS):
            ex[d * N_HEADS + h, h * HEAD_DIM:(h + 1) * HEAD_DIM] = 1.0
        expand.append(jnp.asarray(ex))
    bd_ones = np.kron(np.eye(N_HEADS, dtype=np.float32), np.ones((HEAD_DIM, HEAD_DIM), np.float32))
    ri, ci = np.indices((MIX_BLOCK, MIX_BLOCK))
    same_chunk = (ri // HG_CHUNK) == (ci // HG_CHUNK)
    tri_block = [jnp.asarray(m.astype(np.float32), BF16) for m in (ri >= ci, ri <= ci)]
    tri_chunk = [jnp.asarray((m & same_chunk).astype(np.float32), BF16)
                 for m in (ri >= ci, ri <= ci)]
    return {
        "norm_mix": row(norm_mix[l]), "w_pad": w_pad,
        "rg_conv_w": rg_conv_w[l], "rg_conv_b": row(rg_conv_b[l]),
        "rg_wg": [jnp.concatenate([_block_diag(rg_wa[l, d]), _block_diag(rg_wx[l, d])],
                                  axis=1).astype(BF16) for d in range(2)],
        "rg_bg": [jnp.concatenate([rg_ba[l, d], rg_bx[l, d]]).reshape(1, -1) for d in range(2)],
        "rg_lam": [row(rg_lambda[l, d]) for d in range(2)],
        "hg_lbl": [hg_lb_logits[:, d, :] for d in range(2)], "hg_norm_w": row(hg_norm_w[l]),
        "bd_f32": jnp.asarray(bd_ones), "bd_bf16": jnp.asarray(bd_ones, BF16),
        "tri_block": tri_block, "tri_chunk": tri_chunk,
        "ssd_conv_w": ssd_conv_w[l], "ssd_conv_b": row(ssd_conv_b[l]),
        "ssd_expand": [ex.astype(BF16) for ex in expand],
        "ssd_dt_bias": [rep(ssd_dt_bias[l, d]) for d in range(2)],
        "ssd_a_log": [rep(ssd_a_log[l, d]) for d in range(2)],
        "ssd_d": rep(ssd_d[l]), "ssd_norm_w": row(ssd_norm_w[l]),
        "rope": {s: _rope_tables(s) for s in seqs},
        "w_out": w_out[l].astype(BF16), "norm_ffn": row(norm_ffn[l]),
        "router_w": router_w[l].T,
        "wg": exp_w_gate[l].astype(BF16), "wu": exp_w_up[l].astype(BF16),
        "wd": exp_w_down[l].astype(BF16),
    }


def _trunk(x3, layers, norm_final):
    b, s, _ = x3.shape
    t = b * s
    x = x3.reshape(t, D_MODEL)
    delta = None
    for l, p in enumerate(layers):
        x, (ua, ub, uc, ud) = _inproj(x, delta, p["norm_mix"], p["w_pad"])
        shape3 = lambda u: u.reshape(b, s, u.shape[-1])
        mix = (_rglru(shape3(ua), p, b, s), _hgrn(shape3(ub), p, l, b, s),
               _ssd(shape3(uc), p, b, s), _ret(shape3(ud), p, b, s))
        mix = tuple(m.reshape(t, GROUP_W) for m in mix)
        x, h, aff_t = _outproj(mix, x, p["w_out"], p["norm_ffn"], p["router_w"])
        delta = _expert_choice(h, aff_t, p["wg"], p["wu"], p["wd"])
    return _final(x, delta, norm_final.reshape(1, -1)).reshape(b, s, D_MODEL)


def kernel(x_prompt, x_sample, norm_mix, w_in, rg_conv_w, rg_conv_b, rg_wa, rg_ba, rg_wx, rg_bx, rg_lambda, hg_lb_logits, hg_norm_w, ssd_conv_w, ssd_conv_b, ssd_dt_bias, ssd_a_log, ssd_d, ssd_norm_w, w_out, norm_ffn, router_w, exp_w_gate, exp_w_up, exp_w_down, norm_final):
    seqs = {x_prompt.shape[1], x_sample.shape[1]}
    layers = [_prepare(l, seqs, norm_mix, w_in, rg_conv_w, rg_conv_b, rg_wa, rg_ba, rg_wx, rg_bx,
                       rg_lambda, hg_lb_logits, hg_norm_w, ssd_conv_w, ssd_conv_b, ssd_dt_bias,
                       ssd_a_log, ssd_d, ssd_norm_w, w_out, norm_ffn, router_w, exp_w_gate,
                       exp_w_up, exp_w_down) for l in range(DEPTH)]
    return (_trunk(x_prompt, layers, norm_final), _trunk(x_sample, layers, norm_final))
```

```python
import functools
import math

import jax
import jax.numpy as jnp
import numpy as np
from jax import lax
from jax.experimental import pallas as pl
from jax.experimental.pallas import tpu as pltpu

F32 = jnp.float32
BF16 = jnp.bfloat16

D_MODEL = 1024
DEPTH = 2
GROUP_W = 256
N_HEADS = 4
HEAD_DIM = 64
EPS = 1e-6
RG_C = 8.0
SSD_STATE = 128
SSD_XBC = 768
N_EXPERTS = 16
EXPERT_FF = 2048
EC_CAPACITY_FACTOR = 2
RET_DECAY_EXP = (5.0, 5.5)
ROPE_BASE = 10000.0

LANES = 128
SUBLANES = 8
HALO = SUBLANES
MIX_BLOCK = 256
MIX_BATCH = 8
HG_CHUNK = 64
HG_SUB = 16
PROJ_ROWS = 512
EXP_ROWS = 128
EXP_TOKENS = 4096
VMEM_CAP = 64 * 1024 * 1024

COLS_A = 2 * GROUP_W
COLS_B = 5 * GROUP_W
COLS_C = SSD_XBC + GROUP_W + LANES
COLS_D = 4 * GROUP_W


def _sigmoid(x):
    return 1.0 / (1.0 + jnp.exp(-x))


def _silu(x):
    return x * _sigmoid(x)


def _softplus(x):
    return jnp.maximum(x, 0.0) + jnp.log(1.0 + jnp.exp(-jnp.abs(x)))


def _gelu_tanh(x):
    return 0.5 * x * (1.0 + jnp.tanh(math.sqrt(2.0 / math.pi) * (x + 0.044715 * (x * x * x))))


def _dot(a, b, precision=None):
    return jnp.dot(a, b, preferred_element_type=F32, precision=precision)


def _dot_nt(a, b, precision=None):
    return lax.dot_general(a, b, (((1,), (1,)), ((), ())), preferred_element_type=F32,
                           precision=precision)


def _dot_tn(a, b):
    return lax.dot_general(a, b, (((0,), (0,)), ((), ())), preferred_element_type=F32)


def _head_mask(h, width=GROUP_W, head_dim=HEAD_DIM):
    lane = lax.broadcasted_iota(jnp.int32, (1, width), 1)
    return (lane // head_dim == h).astype(F32).astype(BF16)


def _tri(n, reverse):
    r = lax.broadcasted_iota(jnp.int32, (n, n), 0)
    c = lax.broadcasted_iota(jnp.int32, (n, n), 1)
    return (r <= c) if reverse else (r >= c)


def _vmem(nbytes):
    return int(min(VMEM_CAP - (2 << 20), max(nbytes, 16 << 20)))


def _inproj_body(has_delta, *refs):
    if has_delta:
        x_ref, d_ref, nw_ref, w_ref, xo_ref, ua_ref, ub_ref, uc_ref, ud_ref = refs
        x = x_ref[...] + d_ref[...]
        xo_ref[...] = x
    else:
        x_ref, nw_ref, w_ref, ua_ref, ub_ref, uc_ref, ud_ref = refs
        x = x_ref[...]
    ms = jnp.mean(x * x, axis=-1, keepdims=True)
    h = (x * lax.rsqrt(ms + EPS) * nw_ref[...]).astype(BF16)
    c0 = 0
    for ref, n in ((ua_ref, COLS_A), (ub_ref, COLS_B), (uc_ref, COLS_C), (ud_ref, COLS_D)):
        ref[...] = _dot(h, w_ref[:, c0:c0 + n])
        c0 += n


def _inproj(x, delta, norm_w, w_pad):
    t = x.shape[0]
    tm = min(PROJ_ROWS, t)
    ncols = COLS_A + COLS_B + COLS_C + COLS_D
    row = lambda i: (i, 0)
    fixed = lambda i: (0, 0)
    xspec = pl.BlockSpec((tm, D_MODEL), row)
    in_specs = [xspec] + ([xspec] if delta is not None else []) + [
        pl.BlockSpec((1, D_MODEL), fixed), pl.BlockSpec((D_MODEL, ncols), fixed)]
    u_shapes = [jax.ShapeDtypeStruct((t, n), F32) for n in (COLS_A, COLS_B, COLS_C, COLS_D)]
    u_specs = [pl.BlockSpec((tm, n), row) for n in (COLS_A, COLS_B, COLS_C, COLS_D)]
    out_shape = ([jax.ShapeDtypeStruct((t, D_MODEL), F32)] if delta is not None else []) + u_shapes
    out_specs = ([xspec] if delta is not None else []) + u_specs
    args = (x,) + ((delta,) if delta is not None else ()) + (norm_w, w_pad)
    vm = 2 * (2 * tm * D_MODEL * 4 * 2 + D_MODEL * ncols * 2 + tm * ncols * 4) + (8 << 20)
    outs = pl.pallas_call(
        functools.partial(_inproj_body, delta is not None),
        grid=(t // tm,), in_specs=in_specs, out_specs=out_specs, out_shape=out_shape,
        compiler_params=pltpu.CompilerParams(dimension_semantics=("arbitrary",),
                                             vmem_limit_bytes=_vmem(vm)),
    )(*args)
    if delta is not None:
        return outs[0], outs[1:]
    return x, outs


def _conv4(x, prev8, next8, w, bias, first, last):
    n = x.shape[0]
    pz = jnp.where(first, 0.0, prev8)
    nz = jnp.where(last, 0.0, next8)
    y = (w[0:1] * pltpu.roll(x, 2, 0) + w[1:2] * pltpu.roll(x, 1, 0) + w[2:3] * x
         + w[3:4] * pltpu.roll(x, n - 1, 0) + bias)
    row = lax.broadcasted_iota(jnp.int32, (HALO, x.shape[1]), 0)
    dz = pz - x[n - HALO:]
    dn = nz - x[:HALO]
    top = (jnp.where(row < 2, w[0:1] * pltpu.roll(dz, 2, 0), 0.0)
           + jnp.where(row < 1, w[1:2] * pltpu.roll(dz, 1, 0), 0.0))
    bot = jnp.where(row == HALO - 1, w[3:4] * pltpu.roll(dn, HALO - 1, 0), 0.0)
    return jnp.concatenate([y[:HALO] + top, y[HALO:n - HALO], y[n - HALO:] + bot], axis=0)


def _mixer_call(body, reverse, b, s, row_inputs, halo_inputs, const_inputs, table_inputs,
                out_dtype, state, tables=(), table_init=None, extra_row_inputs=(),
                extra_outputs=()):
    nblk = s // MIX_BLOCK
    hb = MIX_BLOCK // HALO
    nb = math.gcd(b, MIX_BATCH)

    def blk(j):
        return (nblk - 1 - j) if reverse else j

    in_specs, args, per_batch = [], [], []
    for arr, cb, w in tuple(row_inputs) + tuple(extra_row_inputs):
        in_specs.append(pl.BlockSpec((nb, MIX_BLOCK, w), lambda bi, j, cb=cb: (bi, blk(j), cb)))
        args.append(arr)
    for arr, cb, w in halo_inputs:
        in_specs.append(pl.BlockSpec((nb, MIX_BLOCK, w), lambda bi, j, cb=cb: (bi, blk(j), cb)))
        in_specs.append(pl.BlockSpec(
            (nb, HALO, w), lambda bi, j, cb=cb: (bi, jnp.maximum(blk(j) * hb - 1, 0), cb)))
        in_specs.append(pl.BlockSpec(
            (nb, HALO, w),
            lambda bi, j, cb=cb: (bi, jnp.minimum((blk(j) + 1) * hb, s // HALO - 1), cb)))
        args += [arr, arr, arr]
    per_batch += [True] * len(args)
    for arr in table_inputs:
        in_specs.append(pl.BlockSpec((MIX_BLOCK, arr.shape[1]), lambda bi, j: (blk(j), 0)))
        args.append(arr)
    for arr in const_inputs:
        in_specs.append(pl.BlockSpec(arr.shape, lambda bi, j, nd=arr.ndim: (0,) * nd))
        args.append(arr)
    outs = ((GROUP_W, out_dtype),) + tuple(extra_outputs)
    per_batch += [False] * (len(args) - len(per_batch)) + [True] * (len(outs) + len(state))
    per_batch += [False] * len(tables)

    def step(*refs):
        scratch_refs = refs[len(refs) - len(state) - len(tables):]

        @pl.when(pl.program_id(1) == 0)
        def _():
            for r in scratch_refs[:len(state)]:
                r[...] = jnp.zeros_like(r)

        if table_init is not None:
            @pl.when((pl.program_id(0) == 0) & (pl.program_id(1) == 0))
            def _():
                table_init(reverse, *scratch_refs[len(state):])

        for n in range(nb):
            body(reverse, nblk, *[r.at[n] if pb else r for r, pb in zip(refs, per_batch)])

    res = pl.pallas_call(
        step, grid=(b // nb, nblk), in_specs=in_specs,
        out_specs=[pl.BlockSpec((nb, MIX_BLOCK, w), lambda bi, j: (bi, blk(j), 0))
                   for w, _ in outs],
        out_shape=[jax.ShapeDtypeStruct((b, s, w), dt) for w, dt in outs],
        scratch_shapes=([pltpu.VMEM((nb,) + shape, dtype) for shape, dtype in state]
                        + [pltpu.VMEM(shape, dtype) for shape, dtype in tables]),
        compiler_params=pltpu.CompilerParams(dimension_semantics=("arbitrary", "arbitrary"),
                                             vmem_limit_bytes=_vmem(40 << 20)),
    )(*args)
    return res if extra_outputs else res[0]


def _edge_flags(reverse, nblk):
    j = pl.program_id(1)
    jj = (nblk - 1 - j) if reverse else j
    return jj == 0, jj == nblk - 1


def _linear_scan(a, b, carry, reverse):
    n = a.shape[0]
    row = lax.broadcasted_iota(jnp.int32, a.shape, 0) % SUBLANES
    d = 1
    while d < SUBLANES:
        shift = (n - d) if reverse else d
        m = (row < SUBLANES - d) if reverse else (row >= d)
        a_s = pltpu.roll(a, shift, 0)
        b_s = pltpu.roll(b, shift, 0)
        b = jnp.where(m, a * b_s + b, b)
        a = jnp.where(m, a * a_s, a)
        d *= 2
    ngroups = n // SUBLANES
    out = [None] * ngroups
    for g in (range(ngroups - 1, -1, -1) if reverse else range(ngroups)):
        sl = slice(g * SUBLANES, (g + 1) * SUBLANES)
        h = b[sl] + a[sl] * carry
        out[g] = h
        carry = h[0:1] if reverse else h[SUBLANES - 1:SUBLANES]
    return jnp.concatenate(out, axis=0), carry


def _rglru_body(reverse, nblk, *refs):
    if reverse:
        (x_ref, xp_ref, xn_ref, cw_ref, cb_ref, wg_ref, bg_ref, lam_ref, out_ref, u_ref,
         carry) = refs
        first, last = _edge_flags(reverse, nblk)
        u = _conv4(x_ref[...], xp_ref[...], xn_ref[...], cw_ref[...], cb_ref[...], first, last)
        u_ref[...] = u
    else:
        ga_ref, hb_ref, u_ref, wg_ref, bg_ref, lam_ref, out_ref, carry = refs
        u = u_ref[...]
    g = _dot(u.astype(BF16), wg_ref[...]) + bg_ref[...]
    r = _sigmoid(g[:, :GROUP_W])
    i = _sigmoid(g[:, GROUP_W:])
    log_a = -RG_C * r * _softplus(-lam_ref[...])
    a = jnp.exp(log_a)
    inp = jnp.sqrt(1.0 - jnp.exp(2.0 * log_a)) * i * u
    h, carry[...] = _linear_scan(a, inp, carry[...], reverse)
    if reverse:
        out_ref[...] = h
    else:
        out_ref[...] = ((h + hb_ref[...]) * _gelu_tanh(ga_ref[...])).astype(out_ref.dtype)


def _rglru(ua, p, b, s):
    scratch = [((1, GROUP_W), F32)]
    gates = [(p["rg_wg"][d], p["rg_bg"][d], p["rg_lam"][d]) for d in range(2)]
    hb, u = _mixer_call(_rglru_body, True, b, s, (), ((ua, 0, GROUP_W),),
                        (p["rg_conv_w"], p["rg_conv_b"]) + gates[1], (), F32, scratch,
                        extra_outputs=((GROUP_W, F32),))
    return _mixer_call(_rglru_body, False, b, s,
                       ((ua, 1, GROUP_W), (hb, 0, GROUP_W), (u, 0, GROUP_W)), (), gates[0], (),
                       BF16, scratch)


def _split3(x):
    x0 = x.astype(BF16)
    r1 = x - x0.astype(F32)
    x1 = r1.astype(BF16)
    return x0, x1, (r1 - x1.astype(F32)).astype(BF16)


def _cumsum_rows(tri, x):
    return functools.reduce(lambda a, b: a + b, [_dot(tri, piece) for piece in _split3(x)])


def _select_cols(x, sel):
    return functools.reduce(lambda a, b: a + b, [_dot(piece, sel) for piece in _split3(x)])


def _head_mean_sq(o, bd):
    sq = o * o
    hi = sq.astype(BF16)
    lo = (sq - hi.astype(F32)).astype(BF16)
    return (_dot(hi, bd) + _dot(lo, bd)) * (1.0 / HEAD_DIM)


def _gla_chunk(q, k, v, cum, st, bd, reverse):
    c, nsub = HG_CHUNK, HG_CHUNK // HG_SUB
    cum_last = cum[0:1] if reverse else cum[c - 1:c]
    masks = [_head_mask(h) for h in range(N_HEADS)]
    order = list(range(nsub - 1, -1, -1)) if reverse else list(range(nsub))
    entry = {}
    for n_done, i in enumerate(order):
        r0 = i * HG_SUB
        if n_done == 0:
            entry[i] = jnp.zeros((1, GROUP_W), F32)
        else:
            entry[i] = cum[r0 + HG_SUB:r0 + HG_SUB + 1] if reverse else cum[r0 - 1:r0]
    entry_rows = jnp.concatenate(
        [jnp.broadcast_to(entry[i], (HG_SUB, GROUP_W)) for i in range(nsub)], axis=0)
    qh = q * jnp.exp(cum - entry_rows)
    kh = k * jnp.exp(entry_rows - cum)
    khb = kh.astype(BF16)
    kh_bd = jnp.concatenate([khb * m for m in masks], axis=0)
    pairs, lhs = [], []
    for n_done, i in enumerate(order):
        qi = qh[i * HG_SUB:(i + 1) * HG_SUB]
        for j in order[:n_done + 1]:
            pairs.append((i, j))
            lhs.append((qi if j == i else qi * jnp.exp(entry[i] - entry[j])).astype(BF16))
    res = _dot_nt(jnp.concatenate(lhs, axis=0), kh_bd)
    scol = lax.broadcasted_iota(jnp.int32, (HG_SUB, GROUP_W), 1) % HEAD_DIM
    trow = lax.broadcasted_iota(jnp.int32, (HG_SUB, GROUP_W), 0)
    blocks = [None] * nsub
    for n, (i, j) in enumerate(pairs):
        keep = scol // HG_SUB == j
        if i == j:
            keep = keep & ((scol >= trow + i * HG_SUB) if reverse else (scol <= trow + i * HG_SUB))
        piece = jnp.where(keep, res[n * HG_SUB:(n + 1) * HG_SUB], 0.0)
        blocks[i] = piece if blocks[i] is None else blocks[i] + piece
    scores = jnp.concatenate(blocks, axis=0).astype(BF16)
    vb = v.astype(BF16)
    v_bd = jnp.concatenate([vb * m for m in masks], axis=0)
    y = _dot(scores, v_bd)
    y = y + _dot_nt((q * jnp.exp(cum)).astype(BF16), st.astype(BF16))
    kw = (k * jnp.exp(cum_last - cum)).astype(BF16)
    st = st * jnp.exp(cum_last) + bd * _dot_tn(vb, kw)
    return y, st


def _hgrn_body(reverse, nblk, *refs, layer):
    if reverse:
        q_ref, f_ref, i_ref, lbl_ref, tri_ref, bd_ref, out_ref, st_ref = refs
    else:
        (q_ref, f_ref, i_ref, g_ref, ob_ref, lbl_ref, tri_ref, bd_ref, nw_ref, bdb_ref, out_ref,
         st_ref) = refs
    rows = [lbl_ref[r:r + 1, :] for r in range(DEPTH)]
    mx = functools.reduce(jnp.maximum, rows)
    es = [jnp.exp(r - mx) for r in rows]
    tot = functools.reduce(lambda x, y: x + y, es)
    sm = [e / tot for e in es]
    lb = functools.reduce(lambda x, y: x + y, sm[:layer + 1]) - sm[0]

    f = lb + (1.0 - lb) * _sigmoid(f_ref[...])
    lf = jnp.log(f)
    k = 1.0 - f
    q = q_ref[...]
    v = i_ref[...]
    nchunk = MIX_BLOCK // HG_CHUNK
    cum = _cumsum_rows(tri_ref[...], lf)
    bd = bd_ref[...]
    st = st_ref[...]
    ys = [None] * nchunk
    for c in (range(nchunk - 1, -1, -1) if reverse else range(nchunk)):
        sl = slice(c * HG_CHUNK, (c + 1) * HG_CHUNK)
        ys[c], st = _gla_chunk(q[sl], k[sl], v[sl], cum[sl], st, bd, reverse)
    st_ref[...] = st
    y = jnp.concatenate(ys, axis=0)
    if reverse:
        out_ref[...] = y
    else:
        o = y + ob_ref[...]
        o = o * lax.rsqrt(_head_mean_sq(o, bdb_ref[...]) + EPS) * nw_ref[...]
        out_ref[...] = (o * _silu(g_ref[...])).astype(out_ref.dtype)


def _hgrn(ub, p, layer, b, s):
    scratch = [((GROUP_W, GROUP_W), F32)]
    ob = _mixer_call(functools.partial(_hgrn_body, layer=layer), True, b, s,
                     ((ub, 0, GROUP_W), (ub, 2, GROUP_W), (ub, 3, GROUP_W)), (),
                     (p["hg_lbl"][1], p["tri_chunk"][1], p["bd_f32"]), (), F32, scratch)
    return _mixer_call(functools.partial(_hgrn_body, layer=layer), False, b, s,
                       ((ub, 0, GROUP_W), (ub, 1, GROUP_W), (ub, 3, GROUP_W), (ub, 4, GROUP_W),
                        (ob, 0, GROUP_W)), (),
                       (p["hg_lbl"][0], p["tri_chunk"][0], p["bd_f32"], p["hg_norm_w"],
                        p["bd_bf16"]), (), BF16, scratch)


def _ssd_body(reverse, nblk, *refs):
    n = MIX_BLOCK
    if reverse:
        (dt_ref, x_ref, xp_ref, xn_ref, cw_ref, cb_ref, ex_ref, dtb_ref, alog_ref, tri_ref,
         out_ref, xs_ref, bc_ref, st_ref) = refs
        first, last = _edge_flags(reverse, nblk)
        xbc = _silu(_conv4(x_ref[...], xp_ref[...], xn_ref[...], cw_ref[...], cb_ref[...],
                           first, last))
        xs = xbc[:, :GROUP_W]
        bm = xbc[:, GROUP_W:2 * GROUP_W].astype(BF16)
        cm = xbc[:, 2 * GROUP_W:].astype(BF16)
        xs_ref[...] = xs
        bc_ref[:, :GROUP_W] = bm
        bc_ref[:, GROUP_W:] = cm
    else:
        (dt_ref, z_ref, yb_ref, xs_ref, bc_ref, ex_ref, dtb_ref, alog_ref, tri_ref, dsk_ref, nw_ref,
         out_ref, st_ref) = refs
        xs = xs_ref[...]
        bm = bc_ref[:, :GROUP_W]
        cm = bc_ref[:, GROUP_W:]
    dt = _softplus(_select_cols(dt_ref[...], ex_ref[...]) + dtb_ref[...])
    la = dt * (-jnp.exp(alog_ref[...]))
    cum = _cumsum_rows(tri_ref[...], la)
    cum_last = cum[0:1] if reverse else cum[n - 1:n]
    cum_t = cum.T
    v = xs * dt
    vb = v.astype(BF16)
    keep = _tri(n, reverse)
    y = jnp.zeros((n, GROUP_W), F32)
    sc = [_dot_nt(cm[:, g * SSD_STATE:(g + 1) * SSD_STATE],
                  bm[:, g * SSD_STATE:(g + 1) * SSD_STATE]) for g in range(2)]
    for h in range(N_HEADS):
        l0 = h * HEAD_DIM
        seg = cum[:, l0:l0 + 1] - cum_t[l0:l0 + 1, :]
        dec = jnp.exp(jnp.where(keep, seg, -jnp.inf))
        y = y + _dot((sc[h // 2] * dec).astype(BF16), vb * _head_mask(h))
    st = st_ref[...]
    inter = jnp.concatenate(
        [_dot(cm[:, g * SSD_STATE:(g + 1) * SSD_STATE],
              st[:, g * SSD_STATE:(g + 1) * SSD_STATE].astype(BF16)) for g in range(2)], axis=1)
    y = y + jnp.exp(cum) * inter
    vw = (v * jnp.exp(cum_last - cum)).astype(BF16)
    upd = jnp.concatenate(
        [_dot_tn(bm[:, g * SSD_STATE:(g + 1) * SSD_STATE],
                 vw[:, g * SSD_STATE:(g + 1) * SSD_STATE]) for g in range(2)], axis=1)
    st_ref[...] = st * jnp.exp(cum_last) + upd
    if reverse:
        out_ref[...] = y
    else:
        y = (y + yb_ref[...] + dsk_ref[...] * xs) * _silu(z_ref[...])
        ms = jnp.mean(y * y, axis=-1, keepdims=True)
        out_ref[...] = (y * lax.rsqrt(ms + EPS) * nw_ref[...]).astype(out_ref.dtype)


def _ssd(uc, p, b, s):
    scratch = [((SSD_STATE, GROUP_W), F32)]
    dt_cb = (SSD_XBC + GROUP_W) // LANES
    z_cb = SSD_XBC // GROUP_W
    dirs = [(p["ssd_expand"][d], p["ssd_dt_bias"][d], p["ssd_a_log"][d], p["tri_block"][d])
            for d in range(2)]
    yb, xs, bc = _mixer_call(_ssd_body, True, b, s, ((uc, dt_cb, LANES),), ((uc, 0, SSD_XBC),),
                             (p["ssd_conv_w"], p["ssd_conv_b"]) + dirs[1], (), F32, scratch,
                             extra_outputs=((GROUP_W, F32), (2 * GROUP_W, BF16)))
    return _mixer_call(_ssd_body, False, b, s,
                       ((uc, dt_cb, LANES), (uc, z_cb, GROUP_W), (yb, 0, GROUP_W),
                        (xs, 0, GROUP_W), (bc, 0, 2 * GROUP_W)), (),
                       dirs[0] + (p["ssd_d"], p["ssd_norm_w"]), (), BF16, scratch)


def _ret_log_decay(reverse):
    exp0 = RET_DECAY_EXP[1] if reverse else RET_DECAY_EXP[0]
    return [math.log1p(-2.0 ** (-exp0 - h)) for h in range(N_HEADS)]


def _ret_decay_table(reverse, dec_ref):
    n = MIX_BLOCK
    r = lax.broadcasted_iota(jnp.int32, (n, n), 0)
    c = lax.broadcasted_iota(jnp.int32, (n, n), 1)
    dist = (c - r) if reverse else (r - c)
    keep = dist >= 0
    distf = jnp.where(keep, dist, 0).astype(F32)
    for h, lg in enumerate(_ret_log_decay(reverse)):
        dec_ref[h] = jnp.where(keep, jnp.exp(distf * lg), 0.0)


def _ret_body(reverse, nblk, *refs):
    if reverse:
        q_ref, k_ref, v_ref, cos_ref, sin_ref, bd_ref, out_ref, st_ref, dec_ref = refs
    else:
        (q_ref, k_ref, v_ref, g_ref, ob_ref, cos_ref, sin_ref, bd_ref, bdb_ref, out_ref, st_ref,
         dec_ref) = refs
    n = MIX_BLOCK
    log_g = _ret_log_decay(reverse)
    lane = lax.broadcasted_iota(jnp.int32, (n, GROUP_W), 1)
    low_half = (lane % HEAD_DIM) < HEAD_DIM // 2
    cos = cos_ref[...]
    sin = sin_ref[...]

    def rot(x):
        swapped = jnp.where(low_half, pltpu.roll(x, GROUP_W - HEAD_DIM // 2, 1),
                            pltpu.roll(x, HEAD_DIM // 2, 1))
        return x * cos + swapped * sin

    lane_row = lax.broadcasted_iota(jnp.int32, (1, GROUP_W), 1) // HEAD_DIM
    la = functools.reduce(lambda acc, h: jnp.where(lane_row == h, log_g[h], acc),
                          range(N_HEADS), jnp.zeros((1, GROUP_W), F32))
    rowi = lax.broadcasted_iota(jnp.int32, (n, 1), 0)
    steps = ((n - rowi) if reverse else (rowi + 1)).astype(F32)
    cum = steps * la
    cum_last = float(n) * la
    qr = rot(q_ref[...])
    kr = rot(k_ref[...]) * (HEAD_DIM ** -0.5)
    v = v_ref[...]
    qb = qr.astype(BF16)
    kb = kr.astype(BF16)
    vb = v.astype(BF16)
    y = jnp.zeros((n, GROUP_W), F32)
    for h in range(N_HEADS):
        m = _head_mask(h)
        sc = _dot_nt(qb * m, kb)
        y = y + _dot((sc * dec_ref[h]).astype(BF16), vb * m)
    st = st_ref[...]
    y = y + _dot_nt((qr * jnp.exp(cum)).astype(BF16), st.astype(BF16))
    kw = (kr * jnp.exp(cum_last - cum)).astype(BF16)
    st_ref[...] = st * jnp.exp(cum_last) + bd_ref[...] * _dot_tn(vb, kw)
    if reverse:
        out_ref[...] = y
    else:
        o = y + ob_ref[...]
        ms = _head_mean_sq(o, bdb_ref[...])
        out_ref[...] = (o * lax.rsqrt(ms + EPS) * _silu(g_ref[...])).astype(out_ref.dtype)


def _ret(ud, p, b, s):
    state = [((GROUP_W, GROUP_W), F32)]
    tables = [((N_HEADS, MIX_BLOCK, MIX_BLOCK), F32)]
    cos, sin = p["rope"][s]
    ob = _mixer_call(_ret_body, True, b, s,
                     ((ud, 0, GROUP_W), (ud, 1, GROUP_W), (ud, 2, GROUP_W)), (), (p["bd_f32"],),
                     (cos, sin), F32, state, tables, _ret_decay_table)
    return _mixer_call(_ret_body, False, b, s,
                       ((ud, 0, GROUP_W), (ud, 1, GROUP_W), (ud, 2, GROUP_W), (ud, 3, GROUP_W),
                        (ob, 0, GROUP_W)), (), (p["bd_f32"], p["bd_bf16"]), (cos, sin), BF16,
                       state, tables, _ret_decay_table)


def _outproj_body(oa_ref, ob_ref, oc_ref, od_ref, x_ref, w_ref, nw_ref, rw_ref,
                  xo_ref, h_ref, aff_ref):
    acc = x_ref[...]
    for n, ref in enumerate((oa_ref, ob_ref, oc_ref, od_ref)):
        acc = acc + _dot(ref[...], w_ref[n * GROUP_W:(n + 1) * GROUP_W, :])
    xo_ref[...] = acc
    ms = jnp.mean(acc * acc, axis=-1, keepdims=True)
    h = acc * lax.rsqrt(ms + EPS) * nw_ref[...]
    bits = pltpu.bitcast(h.astype(BF16).astype(F32), jnp.uint32)
    half = D_MODEL // 2
    h_ref[...] = (bits[:, :half] >> 16) | (bits[:, half:] & jnp.uint32(0xFFFF0000))
    h0 = h.astype(BF16)
    h1 = (h - h0.astype(F32)).astype(BF16)
    rw = rw_ref[...]
    r0 = rw.astype(BF16)
    r1 = (rw - r0.astype(F32)).astype(BF16)
    logits = _dot_nt(r0, h0) + _dot_nt(r0, h1) + _dot_nt(r1, h0)
    e = jnp.exp(logits - jnp.max(logits, axis=0, keepdims=True))
    aff_ref[...] = e / jnp.sum(e, axis=0, keepdims=True)


def _outproj(mix, x, w_out, norm_w, router_w):
    t = x.shape[0]
    tm = min(PROJ_ROWS, t)
    row = lambda i: (i, 0)
    fixed = lambda i: (0, 0)
    in_specs = [pl.BlockSpec((tm, GROUP_W), row)] * 4 + [
        pl.BlockSpec((tm, D_MODEL), row), pl.BlockSpec((D_MODEL, D_MODEL), fixed),
        pl.BlockSpec((1, D_MODEL), fixed), pl.BlockSpec((N_EXPERTS, D_MODEL), fixed)]
    out_specs = [pl.BlockSpec((tm, D_MODEL), row), pl.BlockSpec((tm, D_MODEL // 2), row),
                 pl.BlockSpec((N_EXPERTS, tm), lambda i: (0, i))]
    out_shape = [jax.ShapeDtypeStruct((t, D_MODEL), F32),
                 jax.ShapeDtypeStruct((t, D_MODEL // 2), jnp.uint32),
                 jax.ShapeDtypeStruct((N_EXPERTS, t), F32)]
    return pl.pallas_call(
        _outproj_body, grid=(t // tm,), in_specs=in_specs, out_specs=out_specs,
        out_shape=out_shape,
        compiler_params=pltpu.CompilerParams(dimension_semantics=("arbitrary",),
                                             vmem_limit_bytes=_vmem(40 << 20)),
    )(*mix, x, w_out, norm_w, router_w)


def _route_body(cap, aff_ref, pos_ref, off_ref):
    a = aff_ref[...]
    ng = a.shape[0]
    bits = pltpu.bitcast(a, jnp.int32)

    def count(m):
        return jnp.sum(jnp.sum(m.astype(F32), axis=0, keepdims=True), axis=1, keepdims=True)

    def step(k, prefix):
        cand = prefix | jnp.left_shift(jnp.int32(1), 30 - k)
        return jnp.where(count(bits >= cand) >= cap, cand, prefix)

    thr = lax.fori_loop(0, 31, step, jnp.zeros((1, 1), jnp.int32))
    r = lax.broadcasted_iota(jnp.int32, (LANES, LANES), 0)
    c = lax.broadcasted_iota(jnp.int32, (LANES, LANES), 1)
    incl = (r <= c).astype(BF16)
    ones = jnp.ones((LANES, LANES), BF16)
    gr = lax.broadcasted_iota(jnp.int32, (ng, ng), 0)
    gc = lax.broadcasted_iota(jnp.int32, (ng, ng), 1)
    before = (gc < gr).astype(BF16)

    def prefix(m):
        mb = m.astype(BF16)
        group_off = _dot(before, _dot(mb, ones).astype(BF16))
        return _dot(mb, incl) - m.astype(F32) + group_off, group_off

    gt = bits > thr
    eq = bits == thr
    need = cap - count(gt)
    eq_rank, _ = prefix(eq)
    sel = gt | (eq & (eq_rank < need))
    pos, group_off = prefix(sel)
    pos_ref[...] = jnp.where(sel, pos, -1.0).astype(jnp.int32)
    off_ref[...] = group_off.astype(jnp.int32)


def _route(aff3, cap):
    ne, ng, _ = aff3.shape
    spec = pl.BlockSpec((None, ng, LANES), lambda e: (e, 0, 0))
    return pl.pallas_call(
        functools.partial(_route_body, cap), grid=(ne,), in_specs=[spec], out_specs=[spec, spec],
        out_shape=[jax.ShapeDtypeStruct(aff3.shape, jnp.int32)] * 2,
        compiler_params=pltpu.CompilerParams(dimension_semantics=("arbitrary",),
                                             vmem_limit_bytes=_vmem(32 << 20)),
    )(aff3)


def _tile_span(off_ref, gpt, ngroups, tile, expert):
    obase = expert * (ngroups + 1) + tile * gpt
    base = off_ref[obase]
    return base, off_ref[obase + gpt] - base, obase


def _lists_body(gpt, ngroups, off_ref, pos_ref, idx_ref):
    nsub = idx_ref.shape[1]
    tok = lax.broadcasted_iota(jnp.int32, (LANES, EXP_ROWS), 0).astype(F32)
    row = lax.broadcasted_iota(jnp.int32, (1, EXP_ROWS), 1)
    idx_ref[...] = jnp.zeros_like(idx_ref)

    def expert(e, carry):
        base, _, obase = _tile_span(off_ref, gpt, ngroups, pl.program_id(0), e)
        pos_t = pos_ref[e].astype(F32).T
        for g in range(gpt):
            first = (off_ref[obase + g] - base) // EXP_ROWS
            pos_g = jnp.broadcast_to(pos_t[:, g:g + 1], (LANES, EXP_ROWS))
            for s in (first, first + 1):
                hit = pos_g == (row + (base + s * EXP_ROWS)).astype(F32)
                add = jnp.sum(jnp.where(hit, tok + float(g * LANES), 0.0), axis=0, keepdims=True)
                idx_ref[e, jnp.minimum(s, nsub - 1)] += add.astype(jnp.int32)
        return carry

    lax.fori_loop(0, idx_ref.shape[0], expert, 0)


def _lists(pos, offsets, tt):
    ne, ngroups, _ = pos.shape
    gpt = tt // LANES
    ntiles = ngroups // gpt
    nsub = tt // EXP_ROWS
    shape = (ne, ntiles, nsub, 1, EXP_ROWS)
    grid_spec = pltpu.PrefetchScalarGridSpec(
        num_scalar_prefetch=1, grid=(ntiles,),
        in_specs=[pl.BlockSpec((ne, gpt, LANES), lambda i, off: (0, i, 0))],
        out_specs=pl.BlockSpec((ne, None, nsub, 1, EXP_ROWS), lambda i, off: (0, i, 0, 0, 0)))
    idx = pl.pallas_call(
        functools.partial(_lists_body, gpt, ngroups), grid_spec=grid_spec,
        out_shape=jax.ShapeDtypeStruct(shape, jnp.int32),
        compiler_params=pltpu.CompilerParams(dimension_semantics=("arbitrary",)),
    )(offsets, pos)
    return idx.reshape(ne, ntiles, 1, tt)


SCATTER_UNROLL = 4


def _expert_body(gpt, ngroups, off_ref, h_ref, idx_ref, gate_ref, wg_ref, wu_ref, wd_ref, y_acc,
                 xe, out):
    @pl.when(pl.program_id(1) == 0)
    def _():
        y_acc[...] = jnp.zeros_like(y_acc)

    @pl.when((pl.program_id(0) == 0) & (pl.program_id(1) == 0))
    def _():
        out[...] = jnp.zeros_like(out)

    _, cnt, _ = _tile_span(off_ref, gpt, ngroups, pl.program_id(0), pl.program_id(1))
    nsub = (cnt + EXP_ROWS - 1) // EXP_ROWS
    last_block = idx_ref.shape[1] // EXP_ROWS - 1

    def gather(dst, s):
        r0 = jnp.minimum(s, last_block) * EXP_ROWS
        for r in range(EXP_ROWS):
            dst[pl.ds(r, 1), :] = h_ref[pl.ds(idx_ref[0, r0 + r], 1), :]

    def ffn(src, dst):
        w = src[...]
        x = jnp.concatenate([pltpu.bitcast(w << 16, F32),
                             pltpu.bitcast(w & jnp.uint32(0xFFFF0000), F32)], axis=1).astype(BF16)
        he = (_silu(_dot(x, wg_ref[...])) * _dot(x, wu_ref[...])).astype(BF16)
        dst[...] = _dot(he, wd_ref[...])

    def scatter(src, s):
        live = s >= 0
        r0 = jnp.maximum(s, 0) * EXP_ROWS
        for k in range(0, EXP_ROWS, SCATTER_UNROLL):
            rows = [k + u for u in range(SCATTER_UNROLL)]
            toks = [idx_ref[0, r0 + r] for r in rows]
            gates = [jnp.where(live & (r0 + r < cnt), gate_ref[0, t], 0.0)
                     for r, t in zip(rows, toks)]
            vals = [y_acc[pl.ds(t, 1), :] + g * src[pl.ds(r, 1), :]
                    for r, t, g in zip(rows, toks, gates)]
            for t, v in reversed(list(zip(toks, vals))):
                y_acc[pl.ds(t, 1), :] = v

    @pl.when(nsub > 0)
    def _():
        gather(xe.at[0], 0)

    def sub_tile(s, carry):
        slot = s & 1
        ffn(xe.at[slot], out.at[slot])
        gather(xe.at[1 - slot], s + 1)
        scatter(out.at[1 - slot], s - 1)
        return carry

    lax.fori_loop(0, nsub, sub_tile, 0)

    @pl.when(nsub > 0)
    def _():
        scatter(out.at[(nsub - 1) & 1], nsub - 1)


def _experts(h, idx, gates, offsets, wg, wu, wd, tt):
    t = h.shape[0]
    gpt = tt // LANES
    ngroups = t // LANES
    half = D_MODEL // 2
    smem_list = pl.BlockSpec((None, None, 1, tt), lambda i, e, off: (e, i, 0, 0),
                             memory_space=pltpu.SMEM)
    grid_spec = pltpu.PrefetchScalarGridSpec(
        num_scalar_prefetch=1, grid=(t // tt, N_EXPERTS),
        in_specs=[
            pl.BlockSpec((tt, half), lambda i, e, off: (i, 0), pipeline_mode=pl.Buffered(1)),
            smem_list, smem_list,
            pl.BlockSpec((None, D_MODEL, EXPERT_FF), lambda i, e, off: (e, 0, 0)),
            pl.BlockSpec((None, D_MODEL, EXPERT_FF), lambda i, e, off: (e, 0, 0)),
            pl.BlockSpec((None, EXPERT_FF, D_MODEL), lambda i, e, off: (e, 0, 0)),
        ],
        out_specs=pl.BlockSpec((tt, D_MODEL), lambda i, e, off: (i, 0),
                               pipeline_mode=pl.Buffered(1)),
        scratch_shapes=[pltpu.VMEM((2, EXP_ROWS, half), jnp.uint32),
                        pltpu.VMEM((2, EXP_ROWS, D_MODEL), F32)],
    )
    vm = (tt * D_MODEL * (2 + 4) + 2 * 3 * D_MODEL * EXPERT_FF * 2
          + EXP_ROWS * (3 * EXPERT_FF + 4 * D_MODEL) * 4 + (4 << 20))
    return pl.pallas_call(
        functools.partial(_expert_body, gpt, ngroups), grid_spec=grid_spec,
        out_shape=jax.ShapeDtypeStruct((t, D_MODEL), F32),
        compiler_params=pltpu.CompilerParams(dimension_semantics=("arbitrary", "arbitrary"),
                                             vmem_limit_bytes=_vmem(vm)),
    )(offsets, h, idx, gates, wg, wu, wd)


def _expert_choice(h, aff_t, wg, wu, wd):
    t = h.shape[0]
    tt = min(EXP_TOKENS, t)
    cap = (EC_CAPACITY_FACTOR * t) // N_EXPERTS
    aff3 = aff_t.reshape(N_EXPERTS, t // LANES, LANES)
    pos, off = _route(aff3, cap)
    offsets = jnp.concatenate([off[:, :, 0], jnp.full((N_EXPERTS, 1), cap, jnp.int32)],
                              axis=1).reshape(-1)
    idx = _lists(pos, offsets, tt)
    gates = aff_t.reshape(N_EXPERTS, t // tt, 1, tt)
    return _experts(h, idx, gates, offsets, wg, wu, wd, tt)


def _final_body(x_ref, d_ref, nw_ref, o_ref):
    x = x_ref[...] + d_ref[...]
    ms = jnp.mean(x * x, axis=-1, keepdims=True)
    o_ref[...] = x * lax.rsqrt(ms + EPS) * nw_ref[...]


def _final(x, delta, norm_w):
    t = x.shape[0]
    tm = min(PROJ_ROWS, t)
    spec = pl.BlockSpec((tm, D_MODEL), lambda i: (i, 0))
    return pl.pallas_call(
        _final_body, grid=(t // tm,),
        in_specs=[spec, spec, pl.BlockSpec((1, D_MODEL), lambda i: (0, 0))], out_specs=spec,
        out_shape=jax.ShapeDtypeStruct((t, D_MODEL), F32),
        compiler_params=pltpu.CompilerParams(dimension_semantics=("arbitrary",)),
    )(x, delta, norm_w)


def _block_diag(w):
    nb, k, j = w.shape
    out = jnp.zeros((nb * k, nb * j), w.dtype)
    for n in range(nb):
        out = out.at[n * k:(n + 1) * k, n * j:(n + 1) * j].set(w[n])
    return out


def _rope_tables(s):
    half = HEAD_DIM // 2
    inv_freq = ROPE_BASE ** (-jnp.arange(half, dtype=F32) / half)
    ang = jnp.arange(s, dtype=F32)[:, None] * inv_freq[None, :]
    cos, sin = jnp.cos(ang), jnp.sin(ang)
    cos_t = jnp.tile(jnp.concatenate([cos, cos], axis=1), (1, N_HEADS))
    sin_t = jnp.tile(jnp.concatenate([-sin, sin], axis=1), (1, N_HEADS))
    return cos_t, sin_t


def _prepare(l, seqs, norm_mix, w_in, rg_conv_w, rg_conv_b, rg_wa, rg_ba, rg_wx, rg_bx, rg_lambda,
             hg_lb_logits, hg_norm_w, ssd_conv_w, ssd_conv_b, ssd_dt_bias, ssd_a_log, ssd_d,
             ssd_norm_w, w_out, norm_ffn, router_w, exp_w_gate, exp_w_up, exp_w_down):
    w = w_in[l]
    a_end = COLS_A
    b_end = a_end + COLS_B
    z0 = b_end
    x0 = z0 + GROUP_W
    dt0 = x0 + SSD_XBC
    d0 = dt0 + 2 * N_HEADS
    w_pad = jnp.concatenate(
        [w[:, :b_end], w[:, x0:dt0], w[:, z0:x0], w[:, dt0:d0],
         jnp.zeros((D_MODEL, LANES - 2 * N_HEADS), w.dtype), w[:, d0:]], axis=1).astype(BF16)
    row = lambda v: v.reshape(1, -1).astype(F32)
    rep = lambda v: jnp.repeat(v, HEAD_DIM).reshape(1, GROUP_W).astype(F32)
    expand = []
    for d in range(2):
        ex = np.zeros((LANES, GROUP_W), np.float32)
        for h in range(N_HEADS):
            ex[d * N_HEADS + h, h * HEAD_DIM:(h + 1) * HEAD_DIM] = 1.0
        expand.append(jnp.asarray(ex))
    bd_ones = np.kron(np.eye(N_HEADS, dtype=np.float32), np.ones((HEAD_DIM, HEAD_DIM), np.float32))
    ri, ci = np.indices((MIX_BLOCK, MIX_BLOCK))
    same_chunk = (ri // HG_CHUNK) == (ci // HG_CHUNK)
    tri_block = [jnp.asarray(m.astype(np.float32), BF16) for m in (ri >= ci, ri <= ci)]
    tri_chunk = [jnp.asarray((m & same_chunk).astype(np.float32), BF16)
                 for m in (ri >= ci, ri <= ci)]
    return {
        "norm_mix": row(norm_mix[l]), "w_pad": w_pad,
        "rg_conv_w": rg_conv_w[l], "rg_conv_b": row(rg_conv_b[l]),
        "rg_wg": [jnp.concatenate([_block_diag(rg_wa[l, d]), _block_diag(rg_wx[l, d])],
                                  axis=1).astype(BF16) for d in range(2)],
        "rg_bg": [jnp.concatenate([rg_ba[l, d], rg_bx[l, d]]).reshape(1, -1) for d in range(2)],
        "rg_lam": [row(rg_lambda[l, d]) for d in range(2)],
        "hg_lbl": [hg_lb_logits[:, d, :] for d in range(2)], "hg_norm_w": row(hg_norm_w[l]),
        "bd_f32": jnp.asarray(bd_ones), "bd_bf16": jnp.asarray(bd_ones, BF16),
        "tri_block": tri_block, "tri_chunk": tri_chunk,
        "ssd_conv_w": ssd_conv_w[l], "ssd_conv_b": row(ssd_conv_b[l]),
        "ssd_expand": [ex.astype(BF16) for ex in expand],
        "ssd_dt_bias": [rep(ssd_dt_bias[l, d]) for d in range(2)],
        "ssd_a_log": [rep(ssd_a_log[l, d]) for d in range(2)],
        "ssd_d": rep(ssd_d[l]), "ssd_norm_w": row(ssd_norm_w[l]),
        "rope": {s: _rope_tables(s) for s in seqs},
        "w_out": w_out[l].astype(BF16), "norm_ffn": row(norm_ffn[l]),
        "router_w": router_w[l].T,
        "wg": exp_w_gate[l].astype(BF16), "wu": exp_w_up[l].astype(BF16),
        "wd": exp_w_down[l].astype(BF16),
    }


def _trunk(x3, layers, norm_final):
    b, s, _ = x3.shape
    t = b * s
    x = x3.reshape(t, D_MODEL)
    delta = None
    for l, p in enumerate(layers):
        x, (ua, ub, uc, ud) = _inproj(x, delta, p["norm_mix"], p["w_pad"])
        shape3 = lambda u: u.reshape(b, s, u.shape[-1])
        mix = (_rglru(shape3(ua), p, b, s), _hgrn(shape3(ub), p, l, b, s),
               _ssd(shape3(uc), p, b, s), _ret(shape3(ud), p, b, s))
        mix = tuple(m.reshape(t, GROUP_W) for m in mix)
        x, h, aff_t = _outproj(mix, x, p["w_out"], p["norm_ffn"], p["router_w"])
        delta = _expert_choice(h, aff_t, p["wg"], p["wu"], p["wd"])
    return _final(x, delta, norm_final.reshape(1, -1)).reshape(b, s, D_MODEL)


def kernel(x_prompt, x_sample, norm_mix, w_in, rg_conv_w, rg_conv_b, rg_wa, rg_ba, rg_wx, rg_bx, rg_lambda, hg_lb_logits, hg_norm_w, ssd_conv_w, ssd_conv_b, ssd_dt_bias, ssd_a_log, ssd_d, ssd_norm_w, w_out, norm_ffn, router_w, exp_w_gate, exp_w_up, exp_w_down, norm_final):
    seqs = {x_prompt.shape[1], x_sample.shape[1]}
    layers = [_prepare(l, seqs, norm_mix, w_in, rg_conv_w, rg_conv_b, rg_wa, rg_ba, rg_wx, rg_bx,
                       rg_lambda, hg_lb_logits, hg_norm_w, ssd_conv_w, ssd_conv_b, ssd_dt_bias,
                       ssd_a_log, ssd_d, ssd_norm_w, w_out, norm_ffn, router_w, exp_w_gate,
                       exp_w_up, exp_w_down) for l in range(DEPTH)]
    return (_trunk(x_prompt, layers, norm_final), _trunk(x_sample, layers, norm_final))
```

```python
import functools
import math

import jax
import jax.numpy as jnp
import numpy as np
from jax import lax
from jax.experimental import pallas as pl
from jax.experimental.pallas import tpu as pltpu

F32 = jnp.float32
BF16 = jnp.bfloat16

D_MODEL = 1024
DEPTH = 2
GROUP_W = 256
N_HEADS = 4
HEAD_DIM = 64
EPS = 1e-6
RG_C = 8.0
SSD_STATE = 128
SSD_XBC = 768
N_EXPERTS = 16
EXPERT_FF = 2048
EC_CAPACITY_FACTOR = 2
RET_DECAY_EXP = (5.0, 5.5)
ROPE_BASE = 10000.0

LANES = 128
SUBLANES = 8
HALO = SUBLANES
MIX_BLOCK = 256
MIX_BATCH = 8
HG_CHUNK = 64
HG_SUB = 16
PROJ_ROWS = 512
EXP_ROWS = 128
EXP_TOKENS = 4096
SCATTER_UNROLL = 4
MIX_VMEM = 40 << 20
VMEM_CAP = 64 * 1024 * 1024

COLS_A = 2 * GROUP_W
COLS_B = 5 * GROUP_W
COLS_C = SSD_XBC + GROUP_W + LANES
COLS_D = 4 * GROUP_W


def _sigmoid(x):
    return 1.0 / (1.0 + jnp.exp(-x))


def _silu(x):
    return x * _sigmoid(x)


def _softplus(x):
    return jnp.maximum(x, 0.0) + jnp.log(1.0 + jnp.exp(-jnp.abs(x)))


def _gelu_tanh(x):
    return 0.5 * x * (1.0 + jnp.tanh(math.sqrt(2.0 / math.pi) * (x + 0.044715 * (x * x * x))))


def _dot(a, b):
    return jnp.dot(a, b, preferred_element_type=F32)


def _dot_nt(a, b):
    return lax.dot_general(a, b, (((1,), (1,)), ((), ())), preferred_element_type=F32)


def _dot_tn(a, b):
    return lax.dot_general(a, b, (((0,), (0,)), ((), ())), preferred_element_type=F32)


def _head_mask(h, width=GROUP_W, head_dim=HEAD_DIM):
    lane = lax.broadcasted_iota(jnp.int32, (1, width), 1)
    return (lane // head_dim == h).astype(F32).astype(BF16)


def _tri(n, reverse):
    r = lax.broadcasted_iota(jnp.int32, (n, n), 0)
    c = lax.broadcasted_iota(jnp.int32, (n, n), 1)
    return (r <= c) if reverse else (r >= c)


def _vmem(nbytes):
    return int(min(VMEM_CAP - (2 << 20), max(nbytes, 16 << 20)))


def _inproj_body(has_delta, *refs):
    if has_delta:
        x_ref, d_ref, nw_ref, w_ref, xo_ref, ua_ref, ub_ref, uc_ref, ud_ref = refs
        x = x_ref[...] + d_ref[...]
        xo_ref[...] = x
    else:
        x_ref, nw_ref, w_ref, ua_ref, ub_ref, uc_ref, ud_ref = refs
        x = x_ref[...]
    ms = jnp.mean(x * x, axis=-1, keepdims=True)
    h = (x * lax.rsqrt(ms + EPS) * nw_ref[...]).astype(BF16)
    c0 = 0
    for ref, n in ((ua_ref, COLS_A), (ub_ref, COLS_B), (uc_ref, COLS_C), (ud_ref, COLS_D)):
        ref[...] = _dot(h, w_ref[:, c0:c0 + n])
        c0 += n


def _inproj(x, delta, norm_w, w_pad):
    t = x.shape[0]
    tm = min(PROJ_ROWS, t)
    ncols = COLS_A + COLS_B + COLS_C + COLS_D
    row = lambda i: (i, 0)
    fixed = lambda i: (0, 0)
    xspec = pl.BlockSpec((tm, D_MODEL), row)
    in_specs = [xspec] + ([xspec] if delta is not None else []) + [
        pl.BlockSpec((1, D_MODEL), fixed), pl.BlockSpec((D_MODEL, ncols), fixed)]
    u_shapes = [jax.ShapeDtypeStruct((t, n), F32) for n in (COLS_A, COLS_B, COLS_C, COLS_D)]
    u_specs = [pl.BlockSpec((tm, n), row) for n in (COLS_A, COLS_B, COLS_C, COLS_D)]
    out_shape = ([jax.ShapeDtypeStruct((t, D_MODEL), F32)] if delta is not None else []) + u_shapes
    out_specs = ([xspec] if delta is not None else []) + u_specs
    args = (x,) + ((delta,) if delta is not None else ()) + (norm_w, w_pad)
    vm = 2 * (2 * tm * D_MODEL * 4 * 2 + D_MODEL * ncols * 2 + tm * ncols * 4) + (8 << 20)
    outs = pl.pallas_call(
        functools.partial(_inproj_body, delta is not None),
        grid=(t // tm,), in_specs=in_specs, out_specs=out_specs, out_shape=out_shape,
        compiler_params=pltpu.CompilerParams(dimension_semantics=("arbitrary",),
                                             vmem_limit_bytes=_vmem(vm)),
    )(*args)
    if delta is not None:
        return outs[0], outs[1:]
    return x, outs


def _conv4(x, prev8, next8, w, bias, first, last):
    n = x.shape[0]
    pz = jnp.where(first, 0.0, prev8)
    nz = jnp.where(last, 0.0, next8)
    y = (w[0:1] * pltpu.roll(x, 2, 0) + w[1:2] * pltpu.roll(x, 1, 0) + w[2:3] * x
         + w[3:4] * pltpu.roll(x, n - 1, 0) + bias)
    row = lax.broadcasted_iota(jnp.int32, (HALO, x.shape[1]), 0)
    dz = pz - x[n - HALO:]
    dn = nz - x[:HALO]
    top = (jnp.where(row < 2, w[0:1] * pltpu.roll(dz, 2, 0), 0.0)
           + jnp.where(row < 1, w[1:2] * pltpu.roll(dz, 1, 0), 0.0))
    bot = jnp.where(row == HALO - 1, w[3:4] * pltpu.roll(dn, HALO - 1, 0), 0.0)
    return jnp.concatenate([y[:HALO] + top, y[HALO:n - HALO], y[n - HALO:] + bot], axis=0)


def _mixer_call(body, reverse, b, s, row_inputs, halo_inputs, const_inputs, table_inputs,
                out_dtype, state, tables=(), table_init=None, extra_outputs=()):
    nblk = s // MIX_BLOCK
    hb = MIX_BLOCK // HALO
    nb = math.gcd(b, MIX_BATCH)

    def blk(j):
        return (nblk - 1 - j) if reverse else j

    in_specs, args, per_batch = [], [], []
    for arr, cb, w in row_inputs:
        in_specs.append(pl.BlockSpec((nb, MIX_BLOCK, w), lambda bi, j, cb=cb: (bi, blk(j), cb)))
        args.append(arr)
    for arr, cb, w in halo_inputs:
        in_specs.append(pl.BlockSpec((nb, MIX_BLOCK, w), lambda bi, j, cb=cb: (bi, blk(j), cb)))
        in_specs.append(pl.BlockSpec(
            (nb, HALO, w), lambda bi, j, cb=cb: (bi, jnp.maximum(blk(j) * hb - 1, 0), cb)))
        in_specs.append(pl.BlockSpec(
            (nb, HALO, w),
            lambda bi, j, cb=cb: (bi, jnp.minimum((blk(j) + 1) * hb, s // HALO - 1), cb)))
        args += [arr, arr, arr]
    per_batch += [True] * len(args)
    for arr in table_inputs:
        in_specs.append(pl.BlockSpec((MIX_BLOCK, arr.shape[1]), lambda bi, j: (blk(j), 0)))
        args.append(arr)
    for arr in const_inputs:
        in_specs.append(pl.BlockSpec(arr.shape, lambda bi, j, nd=arr.ndim: (0,) * nd))
        args.append(arr)
    outs = ((GROUP_W, out_dtype),) + tuple(extra_outputs)
    per_batch += [False] * (len(args) - len(per_batch)) + [True] * (len(outs) + len(state))
    per_batch += [False] * len(tables)

    def step(*refs):
        scratch_refs = refs[len(refs) - len(state) - len(tables):]

        @pl.when(pl.program_id(1) == 0)
        def _():
            for r in scratch_refs[:len(state)]:
                r[...] = jnp.zeros_like(r)

        if table_init is not None:
            @pl.when((pl.program_id(0) == 0) & (pl.program_id(1) == 0))
            def _():
                table_init(reverse, *scratch_refs[len(state):])

        for n in range(nb):
            body(reverse, nblk, *[r.at[n] if pb else r for r, pb in zip(refs, per_batch)])

    res = pl.pallas_call(
        step, grid=(b // nb, nblk), in_specs=in_specs,
        out_specs=[pl.BlockSpec((nb, MIX_BLOCK, w), lambda bi, j: (bi, blk(j), 0))
                   for w, _ in outs],
        out_shape=[jax.ShapeDtypeStruct((b, s, w), dt) for w, dt in outs],
        scratch_shapes=([pltpu.VMEM((nb,) + shape, dtype) for shape, dtype in state]
                        + [pltpu.VMEM(shape, dtype) for shape, dtype in tables]),
        compiler_params=pltpu.CompilerParams(dimension_semantics=("arbitrary", "arbitrary"),
                                             vmem_limit_bytes=_vmem(MIX_VMEM)),
    )(*args)
    return res if extra_outputs else res[0]


def _edge_flags(reverse, nblk):
    j = pl.program_id(1)
    jj = (nblk - 1 - j) if reverse else j
    return jj == 0, jj == nblk - 1


def _linear_scan(a, b, carry, reverse):
    n = a.shape[0]
    row = lax.broadcasted_iota(jnp.int32, a.shape, 0) % SUBLANES
    d = 1
    while d < SUBLANES:
        shift = (n - d) if reverse else d
        m = (row < SUBLANES - d) if reverse else (row >= d)
        a_s = pltpu.roll(a, shift, 0)
        b_s = pltpu.roll(b, shift, 0)
        b = jnp.where(m, a * b_s + b, b)
        a = jnp.where(m, a * a_s, a)
        d *= 2
    ngroups = n // SUBLANES
    out = [None] * ngroups
    for g in (range(ngroups - 1, -1, -1) if reverse else range(ngroups)):
        sl = slice(g * SUBLANES, (g + 1) * SUBLANES)
        h = b[sl] + a[sl] * carry
        out[g] = h
        carry = h[0:1] if reverse else h[SUBLANES - 1:SUBLANES]
    return jnp.concatenate(out, axis=0), carry


def _rglru_body(reverse, nblk, *refs):
    if reverse:
        (x_ref, xp_ref, xn_ref, cw_ref, cb_ref, wg_ref, bg_ref, lam_ref, out_ref, u_ref,
         carry) = refs
        first, last = _edge_flags(reverse, nblk)
        u = _conv4(x_ref[...], xp_ref[...], xn_ref[...], cw_ref[...], cb_ref[...], first, last)
        u_ref[...] = u
    else:
        ga_ref, hb_ref, u_ref, wg_ref, bg_ref, lam_ref, out_ref, carry = refs
        u = u_ref[...]
    g = _dot(u.astype(BF16), wg_ref[...]) + bg_ref[...]
    r = _sigmoid(g[:, :GROUP_W])
    i = _sigmoid(g[:, GROUP_W:])
    log_a = -RG_C * r * _softplus(-lam_ref[...])
    a = jnp.exp(log_a)
    inp = jnp.sqrt(1.0 - jnp.exp(2.0 * log_a)) * i * u
    h, carry[...] = _linear_scan(a, inp, carry[...], reverse)
    if reverse:
        out_ref[...] = h
    else:
        out_ref[...] = ((h + hb_ref[...]) * _gelu_tanh(ga_ref[...])).astype(out_ref.dtype)


def _rglru(ua, p, b, s):
    scratch = [((1, GROUP_W), F32)]
    gates = [(p["rg_wg"][d], p["rg_bg"][d], p["rg_lam"][d]) for d in range(2)]
    hb, u = _mixer_call(_rglru_body, True, b, s, (), ((ua, 0, GROUP_W),),
                        (p["rg_conv_w"], p["rg_conv_b"]) + gates[1], (), F32, scratch,
                        extra_outputs=((GROUP_W, F32),))
    return _mixer_call(_rglru_body, False, b, s,
                       ((ua, 1, GROUP_W), (hb, 0, GROUP_W), (u, 0, GROUP_W)), (), gates[0], (),
                       BF16, scratch)


def _split3(x):
    x0 = x.astype(BF16)
    r1 = x - x0.astype(F32)
    x1 = r1.astype(BF16)
    return x0, x1, (r1 - x1.astype(F32)).astype(BF16)


def _cumsum_rows(tri, x):
    return functools.reduce(lambda a, b: a + b, [_dot(tri, piece) for piece in _split3(x)])


def _select_cols(x, sel):
    return functools.reduce(lambda a, b: a + b, [_dot(piece, sel) for piece in _split3(x)])


def _head_mean_sq(o, bd):
    sq = o * o
    hi = sq.astype(BF16)
    lo = (sq - hi.astype(F32)).astype(BF16)
    return (_dot(hi, bd) + _dot(lo, bd)) * (1.0 / HEAD_DIM)


def _gla_chunk(q, k, v, cum, st, bd, reverse):
    c, nsub = HG_CHUNK, HG_CHUNK // HG_SUB
    cum_last = cum[0:1] if reverse else cum[c - 1:c]
    masks = [_head_mask(h) for h in range(N_HEADS)]
    order = list(range(nsub - 1, -1, -1)) if reverse else list(range(nsub))
    entry = {}
    for n_done, i in enumerate(order):
        r0 = i * HG_SUB
        if n_done == 0:
            entry[i] = jnp.zeros((1, GROUP_W), F32)
        else:
            entry[i] = cum[r0 + HG_SUB:r0 + HG_SUB + 1] if reverse else cum[r0 - 1:r0]
    entry_rows = jnp.concatenate(
        [jnp.broadcast_to(entry[i], (HG_SUB, GROUP_W)) for i in range(nsub)], axis=0)
    qh = q * jnp.exp(cum - entry_rows)
    kh = k * jnp.exp(entry_rows - cum)
    khb = kh.astype(BF16)
    kh_bd = jnp.concatenate([khb * m for m in masks], axis=0)
    pairs, lhs = [], []
    for n_done, i in enumerate(order):
        qi = qh[i * HG_SUB:(i + 1) * HG_SUB]
        for j in order[:n_done + 1]:
            pairs.append((i, j))
            lhs.append((qi if j == i else qi * jnp.exp(entry[i] - entry[j])).astype(BF16))
    res = _dot_nt(jnp.concatenate(lhs, axis=0), kh_bd)
    scol = lax.broadcasted_iota(jnp.int32, (HG_SUB, GROUP_W), 1) % HEAD_DIM
    trow = lax.broadcasted_iota(jnp.int32, (HG_SUB, GROUP_W), 0)
    blocks = [None] * nsub
    for n, (i, j) in enumerate(pairs):
        keep = scol // HG_SUB == j
        if i == j:
            keep = keep & ((scol >= trow + i * HG_SUB) if reverse else (scol <= trow + i * HG_SUB))
        piece = jnp.where(keep, res[n * HG_SUB:(n + 1) * HG_SUB], 0.0)
        blocks[i] = piece if blocks[i] is None else blocks[i] + piece
    scores = jnp.concatenate(blocks, axis=0).astype(BF16)
    vb = v.astype(BF16)
    v_bd = jnp.concatenate([vb * m for m in masks], axis=0)
    y = _dot(scores, v_bd)
    y = y + _dot_nt((q * jnp.exp(cum)).astype(BF16), st.astype(BF16))
    kw = (k * jnp.exp(cum_last - cum)).astype(BF16)
    st = st * jnp.exp(cum_last) + bd * _dot_tn(vb, kw)
    return y, st


def _hgrn_body(reverse, nblk, *refs, layer):
    if reverse:
        q_ref, f_ref, i_ref, lbl_ref, tri_ref, bd_ref, out_ref, st_ref = refs
    else:
        (q_ref, f_ref, i_ref, g_ref, ob_ref, lbl_ref, tri_ref, bd_ref, nw_ref, bdb_ref, out_ref,
         st_ref) = refs
    rows = [lbl_ref[r:r + 1, :] for r in range(DEPTH)]
    mx = functools.reduce(jnp.maximum, rows)
    es = [jnp.exp(r - mx) for r in rows]
    tot = functools.reduce(lambda x, y: x + y, es)
    sm = [e / tot for e in es]
    lb = functools.reduce(lambda x, y: x + y, sm[:layer + 1]) - sm[0]

    f = lb + (1.0 - lb) * _sigmoid(f_ref[...])
    lf = jnp.log(f)
    k = 1.0 - f
    q = q_ref[...]
    v = i_ref[...]
    nchunk = MIX_BLOCK // HG_CHUNK
    cum = _cumsum_rows(tri_ref[...], lf)
    bd = bd_ref[...]
    st = st_ref[...]
    ys = [None] * nchunk
    for c in (range(nchunk - 1, -1, -1) if reverse else range(nchunk)):
        sl = slice(c * HG_CHUNK, (c + 1) * HG_CHUNK)
        ys[c], st = _gla_chunk(q[sl], k[sl], v[sl], cum[sl], st, bd, reverse)
    st_ref[...] = st
    y = jnp.concatenate(ys, axis=0)
    if reverse:
        out_ref[...] = y
    else:
        o = y + ob_ref[...]
        o = o * lax.rsqrt(_head_mean_sq(o, bdb_ref[...]) + EPS) * nw_ref[...]
        out_ref[...] = (o * _silu(g_ref[...])).astype(out_ref.dtype)


def _hgrn(ub, p, layer, b, s):
    scratch = [((GROUP_W, GROUP_W), F32)]
    ob = _mixer_call(functools.partial(_hgrn_body, layer=layer), True, b, s,
                     ((ub, 0, GROUP_W), (ub, 2, GROUP_W), (ub, 3, GROUP_W)), (),
                     (p["hg_lbl"][1], p["tri_chunk"][1], p["bd_f32"]), (), F32, scratch)
    return _mixer_call(functools.partial(_hgrn_body, layer=layer), False, b, s,
                       ((ub, 0, GROUP_W), (ub, 1, GROUP_W), (ub, 3, GROUP_W), (ub, 4, GROUP_W),
                        (ob, 0, GROUP_W)), (),
                       (p["hg_lbl"][0], p["tri_chunk"][0], p["bd_f32"], p["hg_norm_w"],
                        p["bd_bf16"]), (), BF16, scratch)


def _ssd_body(reverse, nblk, *refs):
    n = MIX_BLOCK
    if reverse:
        (dt_ref, x_ref, xp_ref, xn_ref, cw_ref, cb_ref, ex_ref, dtb_ref, alog_ref, tri_ref,
         out_ref, xs_ref, bc_ref, st_ref) = refs
        first, last = _edge_flags(reverse, nblk)
        xbc = _silu(_conv4(x_ref[...], xp_ref[...], xn_ref[...], cw_ref[...], cb_ref[...],
                           first, last))
        xs = xbc[:, :GROUP_W]
        bm = xbc[:, GROUP_W:2 * GROUP_W].astype(BF16)
        cm = xbc[:, 2 * GROUP_W:].astype(BF16)
        xs_ref[...] = xs
        bc_ref[:, :GROUP_W] = bm
        bc_ref[:, GROUP_W:] = cm
    else:
        (dt_ref, z_ref, yb_ref, xs_ref, bc_ref, ex_ref, dtb_ref, alog_ref, tri_ref, dsk_ref, nw_ref,
         out_ref, st_ref) = refs
        xs = xs_ref[...]
        bm = bc_ref[:, :GROUP_W]
        cm = bc_ref[:, GROUP_W:]
    dt = _softplus(_select_cols(dt_ref[...], ex_ref[...]) + dtb_ref[...])
    la = dt * (-jnp.exp(alog_ref[...]))
    cum = _cumsum_rows(tri_ref[...], la)
    cum_last = cum[0:1] if reverse else cum[n - 1:n]
    cum_t = cum.T
    v = xs * dt
    vb = v.astype(BF16)
    keep = _tri(n, reverse)
    y = jnp.zeros((n, GROUP_W), F32)
    sc = [_dot_nt(cm[:, g * SSD_STATE:(g + 1) * SSD_STATE],
                  bm[:, g * SSD_STATE:(g + 1) * SSD_STATE]) for g in range(2)]
    for h in range(N_HEADS):
        l0 = h * HEAD_DIM
        seg = cum[:, l0:l0 + 1] - cum_t[l0:l0 + 1, :]
        dec = jnp.exp(jnp.where(keep, seg, -jnp.inf))
        y = y + _dot((sc[h // 2] * dec).astype(BF16), vb * _head_mask(h))
    st = st_ref[...]
    inter = jnp.concatenate(
        [_dot(cm[:, g * SSD_STATE:(g + 1) * SSD_STATE],
              st[:, g * SSD_STATE:(g + 1) * SSD_STATE].astype(BF16)) for g in range(2)], axis=1)
    y = y + jnp.exp(cum) * inter
    vw = (v * jnp.exp(cum_last - cum)).astype(BF16)
    upd = jnp.concatenate(
        [_dot_tn(bm[:, g * SSD_STATE:(g + 1) * SSD_STATE],
                 vw[:, g * SSD_STATE:(g + 1) * SSD_STATE]) for g in range(2)], axis=1)
    st_ref[...] = st * jnp.exp(cum_last) + upd
    if reverse:
        out_ref[...] = y
    else:
        y = (y + yb_ref[...] + dsk_ref[...] * xs) * _silu(z_ref[...])
        ms = jnp.mean(y * y, axis=-1, keepdims=True)
        out_ref[...] = (y * lax.rsqrt(ms + EPS) * nw_ref[...]).astype(out_ref.dtype)


def _ssd(uc, p, b, s):
    scratch = [((SSD_STATE, GROUP_W), F32)]
    dt_cb = (SSD_XBC + GROUP_W) // LANES
    z_cb = SSD_XBC // GROUP_W
    dirs = [(p["ssd_expand"][d], p["ssd_dt_bias"][d], p["ssd_a_log"][d], p["tri_block"][d])
            for d in range(2)]
    yb, xs, bc = _mixer_call(_ssd_body, True, b, s, ((uc, dt_cb, LANES),), ((uc, 0, SSD_XBC),),
                             (p["ssd_conv_w"], p["ssd_conv_b"]) + dirs[1], (), F32, scratch,
                             extra_outputs=((GROUP_W, F32), (2 * GROUP_W, BF16)))
    return _mixer_call(_ssd_body, False, b, s,
                       ((uc, dt_cb, LANES), (uc, z_cb, GROUP_W), (yb, 0, GROUP_W),
                        (xs, 0, GROUP_W), (bc, 0, 2 * GROUP_W)), (),
                       dirs[0] + (p["ssd_d"], p["ssd_norm_w"]), (), BF16, scratch)


def _ret_log_decay(reverse):
    exp0 = RET_DECAY_EXP[1] if reverse else RET_DECAY_EXP[0]
    return [math.log1p(-2.0 ** (-exp0 - h)) for h in range(N_HEADS)]


def _ret_decay_table(reverse, dec_ref):
    n = MIX_BLOCK
    r = lax.broadcasted_iota(jnp.int32, (n, n), 0)
    c = lax.broadcasted_iota(jnp.int32, (n, n), 1)
    dist = (c - r) if reverse else (r - c)
    keep = dist >= 0
    distf = jnp.where(keep, dist, 0).astype(F32)
    for h, lg in enumerate(_ret_log_decay(reverse)):
        dec_ref[h] = jnp.where(keep, jnp.exp(distf * lg), 0.0)


def _ret_body(reverse, nblk, *refs):
    if reverse:
        q_ref, k_ref, v_ref, cos_ref, sin_ref, bd_ref, out_ref, st_ref, dec_ref = refs
    else:
        (q_ref, k_ref, v_ref, g_ref, ob_ref, cos_ref, sin_ref, bd_ref, bdb_ref, out_ref, st_ref,
         dec_ref) = refs
    n = MIX_BLOCK
    log_g = _ret_log_decay(reverse)
    lane = lax.broadcasted_iota(jnp.int32, (n, GROUP_W), 1)
    low_half = (lane % HEAD_DIM) < HEAD_DIM // 2
    cos = cos_ref[...]
    sin = sin_ref[...]

    def rot(x):
        swapped = jnp.where(low_half, pltpu.roll(x, GROUP_W - HEAD_DIM // 2, 1),
                            pltpu.roll(x, HEAD_DIM // 2, 1))
        return x * cos + swapped * sin

    lane_row = lax.broadcasted_iota(jnp.int32, (1, GROUP_W), 1) // HEAD_DIM
    la = functools.reduce(lambda acc, h: jnp.where(lane_row == h, log_g[h], acc),
                          range(N_HEADS), jnp.zeros((1, GROUP_W), F32))
    rowi = lax.broadcasted_iota(jnp.int32, (n, 1), 0)
    steps = ((n - rowi) if reverse else (rowi + 1)).astype(F32)
    cum = steps * la
    cum_last = float(n) * la
    qr = rot(q_ref[...])
    kr = rot(k_ref[...]) * (HEAD_DIM ** -0.5)
    v = v_ref[...]
    qb = qr.astype(BF16)
    kb = kr.astype(BF16)
    vb = v.astype(BF16)
    y = jnp.zeros((n, GROUP_W), F32)
    for h in range(N_HEADS):
        m = _head_mask(h)
        sc = _dot_nt(qb * m, kb)
        y = y + _dot((sc * dec_ref[h]).astype(BF16), vb * m)
    st = st_ref[...]
    y = y + _dot_nt((qr * jnp.exp(cum)).astype(BF16), st.astype(BF16))
    kw = (kr * jnp.exp(cum_last - cum)).astype(BF16)
    st_ref[...] = st * jnp.exp(cum_last) + bd_ref[...] * _dot_tn(vb, kw)
    if reverse:
        out_ref[...] = y
    else:
        o = y + ob_ref[...]
        ms = _head_mean_sq(o, bdb_ref[...])
        out_ref[...] = (o * lax.rsqrt(ms + EPS) * _silu(g_ref[...])).astype(out_ref.dtype)


def _ret(ud, p, b, s):
    state = [((GROUP_W, GROUP_W), F32)]
    tables = [((N_HEADS, MIX_BLOCK, MIX_BLOCK), F32)]
    cos, sin = p["rope"][s]
    ob = _mixer_call(_ret_body, True, b, s,
                     ((ud, 0, GROUP_W), (ud, 1, GROUP_W), (ud, 2, GROUP_W)), (), (p["bd_f32"],),
                     (cos, sin), F32, state, tables, _ret_decay_table)
    return _mixer_call(_ret_body, False, b, s,
                       ((ud, 0, GROUP_W), (ud, 1, GROUP_W), (ud, 2, GROUP_W), (ud, 3, GROUP_W),
                        (ob, 0, GROUP_W)), (), (p["bd_f32"], p["bd_bf16"]), (cos, sin), BF16,
                       state, tables, _ret_decay_table)


def _outproj_body(oa_ref, ob_ref, oc_ref, od_ref, x_ref, w_ref, nw_ref, rw_ref,
                  xo_ref, h_ref, aff_ref):
    acc = x_ref[...]
    for n, ref in enumerate((oa_ref, ob_ref, oc_ref, od_ref)):
        acc = acc + _dot(ref[...], w_ref[n * GROUP_W:(n + 1) * GROUP_W, :])
    xo_ref[...] = acc
    ms = jnp.mean(acc * acc, axis=-1, keepdims=True)
    h = acc * lax.rsqrt(ms + EPS) * nw_ref[...]
    bits = pltpu.bitcast(h.astype(BF16).astype(F32), jnp.uint32)
    half = D_MODEL // 2
    h_ref[...] = (bits[:, :half] >> 16) | (bits[:, half:] & jnp.uint32(0xFFFF0000))
    h0 = h.astype(BF16)
    h1 = (h - h0.astype(F32)).astype(BF16)
    rw = rw_ref[...]
    r0 = rw.astype(BF16)
    r1 = (rw - r0.astype(F32)).astype(BF16)
    logits = _dot_nt(r0, h0) + _dot_nt(r0, h1) + _dot_nt(r1, h0)
    e = jnp.exp(logits - jnp.max(logits, axis=0, keepdims=True))
    aff_ref[...] = e / jnp.sum(e, axis=0, keepdims=True)


def _outproj(mix, x, w_out, norm_w, router_w):
    t = x.shape[0]
    tm = min(PROJ_ROWS, t)
    row = lambda i: (i, 0)
    fixed = lambda i: (0, 0)
    in_specs = [pl.BlockSpec((tm, GROUP_W), row)] * 4 + [
        pl.BlockSpec((tm, D_MODEL), row), pl.BlockSpec((D_MODEL, D_MODEL), fixed),
        pl.BlockSpec((1, D_MODEL), fixed), pl.BlockSpec((N_EXPERTS, D_MODEL), fixed)]
    out_specs = [pl.BlockSpec((tm, D_MODEL), row), pl.BlockSpec((tm, D_MODEL // 2), row),
                 pl.BlockSpec((N_EXPERTS, tm), lambda i: (0, i))]
    out_shape = [jax.ShapeDtypeStruct((t, D_MODEL), F32),
                 jax.ShapeDtypeStruct((t, D_MODEL // 2), jnp.uint32),
                 jax.ShapeDtypeStruct((N_EXPERTS, t), F32)]
    vm = 4 * tm * (4 * GROUP_W * 2 + 2 * D_MODEL * 4 + D_MODEL * 2 + N_EXPERTS * 4) \
        + 4 * D_MODEL * D_MODEL * 2
    return pl.pallas_call(
        _outproj_body, grid=(t // tm,), in_specs=in_specs, out_specs=out_specs,
        out_shape=out_shape,
        compiler_params=pltpu.CompilerParams(dimension_semantics=("arbitrary",),
                                             vmem_limit_bytes=_vmem(vm)),
    )(*mix, x, w_out, norm_w, router_w)


def _route_body(cap, aff_ref, pos_ref, off_ref):
    a = aff_ref[...]
    ng = a.shape[0]
    bits = pltpu.bitcast(a, jnp.int32)

    def count(m):
        return jnp.sum(jnp.sum(m.astype(F32), axis=0, keepdims=True), axis=1, keepdims=True)

    def step(k, prefix):
        cand = prefix | jnp.left_shift(jnp.int32(1), 30 - k)
        return jnp.where(count(bits >= cand) >= cap, cand, prefix)

    thr = lax.fori_loop(0, 31, step, jnp.zeros((1, 1), jnp.int32))
    r = lax.broadcasted_iota(jnp.int32, (LANES, LANES), 0)
    c = lax.broadcasted_iota(jnp.int32, (LANES, LANES), 1)
    incl = (r <= c).astype(BF16)
    ones = jnp.ones((LANES, LANES), BF16)
    gr = lax.broadcasted_iota(jnp.int32, (ng, ng), 0)
    gc = lax.broadcasted_iota(jnp.int32, (ng, ng), 1)
    before = (gc < gr).astype(BF16)

    def prefix(m):
        mb = m.astype(BF16)
        group_off = _dot(before, _dot(mb, ones).astype(BF16))
        return _dot(mb, incl) - m.astype(F32) + group_off, group_off

    gt = bits > thr
    eq = bits == thr
    need = cap - count(gt)
    eq_rank, _ = prefix(eq)
    sel = gt | (eq & (eq_rank < need))
    pos, group_off = prefix(sel)
    pos_ref[...] = jnp.where(sel, pos, -1.0).astype(jnp.int32)
    off_ref[...] = group_off.astype(jnp.int32)


def _route(aff3, cap):
    ne, ng, _ = aff3.shape
    spec = pl.BlockSpec((None, ng, LANES), lambda e: (e, 0, 0))
    return pl.pallas_call(
        functools.partial(_route_body, cap), grid=(ne,), in_specs=[spec], out_specs=[spec, spec],
        out_shape=[jax.ShapeDtypeStruct(aff3.shape, jnp.int32)] * 2,
        compiler_params=pltpu.CompilerParams(
            dimension_semantics=("arbitrary",),
            vmem_limit_bytes=_vmem(24 * ng * LANES * 4 + 4 * ng * ng * 2)),
    )(aff3)


def _tile_span(off_ref, gpt, ngroups, tile, expert):
    obase = expert * (ngroups + 1) + tile * gpt
    base = off_ref[obase]
    return base, off_ref[obase + gpt] - base, obase


def _lists_body(gpt, ngroups, off_ref, pos_ref, idx_ref):
    nsub = idx_ref.shape[1]
    tok = lax.broadcasted_iota(jnp.int32, (LANES, EXP_ROWS), 0).astype(F32)
    row = lax.broadcasted_iota(jnp.int32, (1, EXP_ROWS), 1)
    idx_ref[...] = jnp.zeros_like(idx_ref)

    def expert(e, carry):
        base, _, obase = _tile_span(off_ref, gpt, ngroups, pl.program_id(0), e)
        pos_t = pos_ref[e].astype(F32).T
        for g in range(gpt):
            first = (off_ref[obase + g] - base) // EXP_ROWS
            pos_g = jnp.broadcast_to(pos_t[:, g:g + 1], (LANES, EXP_ROWS))
            for s in (first, first + 1):
                hit = pos_g == (row + (base + s * EXP_ROWS)).astype(F32)
                add = jnp.sum(jnp.where(hit, tok + float(g * LANES), 0.0), axis=0, keepdims=True)
                idx_ref[e, jnp.minimum(s, nsub - 1)] += add.astype(jnp.int32)
        return carry

    lax.fori_loop(0, idx_ref.shape[0], expert, 0)


def _lists(pos, offsets, tt):
    ne, ngroups, _ = pos.shape
    gpt = tt // LANES
    ntiles = ngroups // gpt
    nsub = tt // EXP_ROWS
    shape = (ne, ntiles, nsub, 1, EXP_ROWS)
    grid_spec = pltpu.PrefetchScalarGridSpec(
        num_scalar_prefetch=1, grid=(ntiles,),
        in_specs=[pl.BlockSpec((ne, gpt, LANES), lambda i, off: (0, i, 0))],
        out_specs=pl.BlockSpec((ne, None, nsub, 1, EXP_ROWS), lambda i, off: (0, i, 0, 0, 0)))
    idx = pl.pallas_call(
        functools.partial(_lists_body, gpt, ngroups), grid_spec=grid_spec,
        out_shape=jax.ShapeDtypeStruct(shape, jnp.int32),
        compiler_params=pltpu.CompilerParams(dimension_semantics=("arbitrary",)),
    )(offsets, pos)
    return idx.reshape(ne, ntiles, 1, tt)


def _expert_body(gpt, ngroups, off_ref, h_ref, idx_ref, gate_ref, wg_ref, wu_ref, wd_ref, y_acc,
                 xe, out):
    @pl.when(pl.program_id(1) == 0)
    def _():
        y_acc[...] = jnp.zeros_like(y_acc)

    @pl.when((pl.program_id(0) == 0) & (pl.program_id(1) == 0))
    def _():
        out[...] = jnp.zeros_like(out)

    _, cnt, _ = _tile_span(off_ref, gpt, ngroups, pl.program_id(0), pl.program_id(1))
    nsub = (cnt + EXP_ROWS - 1) // EXP_ROWS
    last_block = idx_ref.shape[1] // EXP_ROWS - 1

    def gather(dst, s):
        r0 = jnp.minimum(s, last_block) * EXP_ROWS
        for r in range(EXP_ROWS):
            dst[pl.ds(r, 1), :] = h_ref[pl.ds(idx_ref[0, r0 + r], 1), :]

    def ffn(src, dst):
        w = src[...]
        x = jnp.concatenate([pltpu.bitcast(w << 16, F32),
                             pltpu.bitcast(w & jnp.uint32(0xFFFF0000), F32)], axis=1).astype(BF16)
        he = (_silu(_dot(x, wg_ref[...])) * _dot(x, wu_ref[...])).astype(BF16)
        dst[...] = _dot(he, wd_ref[...])

    def scatter(src, s):
        r0 = jnp.maximum(s, 0) * EXP_ROWS
        nlive = jnp.where(s >= 0, cnt - r0, 0)
        for k in range(0, EXP_ROWS, SCATTER_UNROLL):
            rows = [k + u for u in range(SCATTER_UNROLL)]
            toks = [idx_ref[0, r0 + r] for r in rows]
            gates = [jnp.where(r < nlive, gate_ref[0, t], 0.0) for r, t in zip(rows, toks)]
            vals = [y_acc[pl.ds(t, 1), :] + g * src[pl.ds(r, 1), :]
                    for r, t, g in zip(rows, toks, gates)]
            for t, v in reversed(list(zip(toks, vals))):
                y_acc[pl.ds(t, 1), :] = v

    @pl.when(nsub > 0)
    def _():
        gather(xe.at[0], 0)

    def sub_tile(s, carry):
        slot = s & 1
        ffn(xe.at[slot], out.at[slot])
        gather(xe.at[1 - slot], s + 1)
        scatter(out.at[1 - slot], s - 1)
        return carry

    lax.fori_loop(0, nsub, sub_tile, 0)

    @pl.when(nsub > 0)
    def _():
        scatter(out.at[(nsub - 1) & 1], nsub - 1)


def _experts(h, idx, gates, offsets, wg, wu, wd, tt):
    t = h.shape[0]
    gpt = tt // LANES
    ngroups = t // LANES
    half = D_MODEL // 2
    smem_list = pl.BlockSpec((None, None, 1, tt), lambda i, e, off: (e, i, 0, 0),
                             memory_space=pltpu.SMEM)
    grid_spec = pltpu.PrefetchScalarGridSpec(
        num_scalar_prefetch=1, grid=(t // tt, N_EXPERTS),
        in_specs=[
            pl.BlockSpec((tt, half), lambda i, e, off: (i, 0), pipeline_mode=pl.Buffered(1)),
            smem_list, smem_list,
            pl.BlockSpec((None, D_MODEL, EXPERT_FF), lambda i, e, off: (e, 0, 0)),
            pl.BlockSpec((None, D_MODEL, EXPERT_FF), lambda i, e, off: (e, 0, 0)),
            pl.BlockSpec((None, EXPERT_FF, D_MODEL), lambda i, e, off: (e, 0, 0)),
        ],
        out_specs=pl.BlockSpec((tt, D_MODEL), lambda i, e, off: (i, 0),
                               pipeline_mode=pl.Buffered(1)),
        scratch_shapes=[pltpu.VMEM((2, EXP_ROWS, half), jnp.uint32),
                        pltpu.VMEM((2, EXP_ROWS, D_MODEL), F32)],
    )
    vm = (tt * D_MODEL * (2 + 4) + 2 * 3 * D_MODEL * EXPERT_FF * 2
          + EXP_ROWS * (3 * EXPERT_FF + 4 * D_MODEL) * 4 + (4 << 20))
    return pl.pallas_call(
        functools.partial(_expert_body, gpt, ngroups), grid_spec=grid_spec,
        out_shape=jax.ShapeDtypeStruct((t, D_MODEL), F32),
        compiler_params=pltpu.CompilerParams(dimension_semantics=("arbitrary", "arbitrary"),
                                             vmem_limit_bytes=_vmem(vm)),
    )(offsets, h, idx, gates, wg, wu, wd)


def _expert_choice(h, aff_t, wg, wu, wd):
    t = h.shape[0]
    tt = min(EXP_TOKENS, t)
    cap = (EC_CAPACITY_FACTOR * t) // N_EXPERTS
    aff3 = aff_t.reshape(N_EXPERTS, t // LANES, LANES)
    pos, off = _route(aff3, cap)
    offsets = jnp.concatenate([off[:, :, 0], jnp.full((N_EXPERTS, 1), cap, jnp.int32)],
                              axis=1).reshape(-1)
    idx = _lists(pos, offsets, tt)
    gates = aff_t.reshape(N_EXPERTS, t // tt, 1, tt)
    return _experts(h, idx, gates, offsets, wg, wu, wd, tt)


def _final_body(x_ref, d_ref, nw_ref, o_ref):
    x = x_ref[...] + d_ref[...]
    ms = jnp.mean(x * x, axis=-1, keepdims=True)
    o_ref[...] = x * lax.rsqrt(ms + EPS) * nw_ref[...]


def _final(x, delta, norm_w):
    t = x.shape[0]
    tm = min(PROJ_ROWS, t)
    spec = pl.BlockSpec((tm, D_MODEL), lambda i: (i, 0))
    return pl.pallas_call(
        _final_body, grid=(t // tm,),
        in_specs=[spec, spec, pl.BlockSpec((1, D_MODEL), lambda i: (0, 0))], out_specs=spec,
        out_shape=jax.ShapeDtypeStruct((t, D_MODEL), F32),
        compiler_params=pltpu.CompilerParams(dimension_semantics=("arbitrary",)),
    )(x, delta, norm_w)


def _block_diag(w):
    nb, k, j = w.shape
    out = jnp.zeros((nb * k, nb * j), w.dtype)
    for n in range(nb):
        out = out.at[n * k:(n + 1) * k, n * j:(n + 1) * j].set(w[n])
    return out


def _rope_tables(s):
    half = HEAD_DIM // 2
    inv_freq = ROPE_BASE ** (-jnp.arange(half, dtype=F32) / half)
    ang = jnp.arange(s, dtype=F32)[:, None] * inv_freq[None, :]
    cos, sin = jnp.cos(ang), jnp.sin(ang)
    cos_t = jnp.tile(jnp.concatenate([cos, cos], axis=1), (1, N_HEADS))
    sin_t = jnp.tile(jnp.concatenate([-sin, sin], axis=1), (1, N_HEADS))
    return cos_t, sin_t


def _prepare(l, seqs, norm_mix, w_in, rg_conv_w, rg_conv_b, rg_wa, rg_ba, rg_wx, rg_bx, rg_lambda,
             hg_lb_logits, hg_norm_w, ssd_conv_w, ssd_conv_b, ssd_dt_bias, ssd_a_log, ssd_d,
             ssd_norm_w, w_out, norm_ffn, router_w, exp_w_gate, exp_w_up, exp_w_down):
    w = w_in[l]
    a_end = COLS_A
    b_end = a_end + COLS_B
    z0 = b_end
    x0 = z0 + GROUP_W
    dt0 = x0 + SSD_XBC
    d0 = dt0 + 2 * N_HEADS
    w_pad = jnp.concatenate(
        [w[:, :b_end], w[:, x0:dt0], w[:, z0:x0], w[:, dt0:d0],
         jnp.zeros((D_MODEL, LANES - 2 * N_HEADS), w.dtype), w[:, d0:]], axis=1).astype(BF16)
    row = lambda v: v.reshape(1, -1).astype(F32)
    rep = lambda v: jnp.repeat(v, HEAD_DIM).reshape(1, GROUP_W).astype(F32)
    expand = []
    for d in range(2):
        ex = np.zeros((LANES, GROUP_W), np.float32)
        for h in range(N_HEADS):
            ex[d * N_HEADS + h, h * HEAD_DIM:(h + 1) * HEAD_DIM] = 1.0
        expand.append(jnp.asarray(ex))
    bd_ones = np.kron(np.eye(N_HEADS, dtype=np.float32), np.ones((HEAD_DIM, HEAD_DIM), np.float32))
    ri, ci = np.indices((MIX_BLOCK, MIX_BLOCK))
    same_chunk = (ri // HG_CHUNK) == (ci // HG_CHUNK)
    tri_block = [jnp.asarray(m.astype(np.float32), BF16) for m in (ri >= ci, ri <= ci)]
    tri_chunk = [jnp.asarray((m & same_chunk).astype(np.float32), BF16)
                 for m in (ri >= ci, ri <= ci)]
    return {
        "norm_mix": row(norm_mix[l]), "w_pad": w_pad,
        "rg_conv_w": rg_conv_w[l], "rg_conv_b": row(rg_conv_b[l]),
        "rg_wg": [jnp.concatenate([_block_diag(rg_wa[l, d]), _block_diag(rg_wx[l, d])],
                                  axis=1).astype(BF16) for d in range(2)],
        "rg_bg": [jnp.concatenate([rg_ba[l, d], rg_bx[l, d]]).reshape(1, -1) for d in range(2)],
        "rg_lam": [row(rg_lambda[l, d]) for d in range(2)],
        "hg_lbl": [hg_lb_logits[:, d, :] for d in range(2)], "hg_norm_w": row(hg_norm_w[l]),
        "bd_f32": jnp.asarray(bd_ones), "bd_bf16": jnp.asarray(bd_ones, BF16),
        "tri_block": tri_block, "tri_chunk": tri_chunk,
        "ssd_conv_w": ssd_conv_w[l], "ssd_conv_b": row(ssd_conv_b[l]),
        "ssd_expand": [ex.astype(BF16) for ex in expand],
        "ssd_dt_bias": [rep(ssd_dt_bias[l, d]) for d in range(2)],
        "ssd_a_log": [rep(ssd_a_log[l, d]) for d in range(2)],
        "ssd_d": rep(ssd_d[l]), "ssd_norm_w": row(ssd_norm_w[l]),
        "rope": {s: _rope_tables(s) for s in seqs},
        "w_out": w_out[l].astype(BF16), "norm_ffn": row(norm_ffn[l]),
        "router_w": router_w[l].T,
        "wg": exp_w_gate[l].astype(BF16), "wu": exp_w_up[l].astype(BF16),
        "wd": exp_w_down[l].astype(BF16),
    }


def _trunk(x3, layers, norm_final):
    b, s, _ = x3.shape
    t = b * s
    x = x3.reshape(t, D_MODEL)
    delta = None
    for l, p in enumerate(layers):
        x, (ua, ub, uc, ud) = _inproj(x, delta, p["norm_mix"], p["w_pad"])
        shape3 = lambda u: u.reshape(b, s, u.shape[-1])
        mix = (_rglru(shape3(ua), p, b, s), _hgrn(shape3(ub), p, l, b, s),
               _ssd(shape3(uc), p, b, s), _ret(shape3(ud), p, b, s))
        mix = tuple(m.reshape(t, GROUP_W) for m in mix)
        x, h, aff_t = _outproj(mix, x, p["w_out"], p["norm_ffn"], p["router_w"])
        delta = _expert_choice(h, aff_t, p["wg"], p["wu"], p["wd"])
    return _final(x, delta, norm_final.reshape(1, -1)).reshape(b, s, D_MODEL)


def kernel(x_prompt, x_sample, norm_mix, w_in, rg_conv_w, rg_conv_b, rg_wa, rg_ba, rg_wx, rg_bx, rg_lambda, hg_lb_logits, hg_norm_w, ssd_conv_w, ssd_conv_b, ssd_dt_bias, ssd_a_log, ssd_d, ssd_norm_w, w_out, norm_ffn, router_w, exp_w_gate, exp_w_up, exp_w_down, norm_final):
    seqs = {x_prompt.shape[1], x_sample.shape[1]}
    layers = [_prepare(l, seqs, norm_mix, w_in, rg_conv_w, rg_conv_b, rg_wa, rg_ba, rg_wx, rg_bx,
                       rg_lambda, hg_lb_logits, hg_norm_w, ssd_conv_w, ssd_conv_b, ssd_dt_bias,
                       ssd_a_log, ssd_d, ssd_norm_w, w_out, norm_ffn, router_w, exp_w_gate,
                       exp_w_up, exp_w_down) for l in range(DEPTH)]
    return (_trunk(x_prompt, layers, norm_final), _trunk(x_sample, layers, norm_final))
```
